```python
import math
import jax, jax.numpy as jnp
from jax import lax
import numpy as np

D_MODEL = 4096
BATCH = 2
SEQ = 4096
DEPTH = 1

MEM_LEN = 256
MIX_WIDTH = D_MODEL
MLSTM_HEADS = 4
MLSTM_V = MIX_WIDTH // 2
MLSTM_DV = MLSTM_V // MLSTM_HEADS
MLSTM_DQK = MLSTM_DV // 2
MLSTM_QK = MLSTM_HEADS * MLSTM_DQK
MLSTM_CHUNK = 64
GATE_CAP = 15.0
CONV_WIDTH = MIX_WIDTH // 2
CONV_K = 3
N_BRANCH = 2
XATTN_HEADS = 4
XATTN_DH = D_MODEL // XATTN_HEADS
D_FF = ((8 * D_MODEL // 3 + 255) // 256) * 256
IN_WIDTH = 2 * MLSTM_QK + 2 * MLSTM_V + 2 * MLSTM_HEADS + 3 * CONV_WIDTH + N_BRANCH * D_MODEL
EPS = 1e-6

kernel_name = "hybrid_mlstm_shortconv_gated_block"


def rms_norm(x, w):
    xf = x.astype(jnp.float32)
    y = xf * lax.rsqrt(jnp.mean(xf * xf, axis=-1, keepdims=True) + EPS)
    return (y * w.astype(jnp.float32)).astype(x.dtype)


def mlstm_chunkwise(q, k, v, i_pre, f_pre):
    out_dtype = v.dtype
    B, S, H, dqk = q.shape
    dv = v.shape[-1]
    L = MLSTM_CHUNK
    nc = S // L
    q = q.astype(jnp.float32) * (dqk ** -0.5)
    k = k.astype(jnp.float32)
    v = v.astype(jnp.float32)
    log_i = i_pre.astype(jnp.float32)
    log_f = jax.nn.log_sigmoid(f_pre.astype(jnp.float32))

    def to_chunks(t):
        return t.reshape(B, nc, L, H, t.shape[-1]).transpose(1, 0, 3, 2, 4)

    def gate_chunks(g):
        return g.reshape(B, nc, L, H).transpose(1, 0, 3, 2)

    mask = jnp.tril(jnp.ones((L, L), dtype=bool))

    def step(carry, xs):
        C, n, m = carry
        qc, kc, vc, li, lf = xs
        b = jnp.cumsum(lf, axis=-1)
        Dm = b[..., :, None] - b[..., None, :] + li[..., None, :]
        Dm = jnp.where(mask, Dm, -jnp.inf)
        inter = b + m[..., None]
        m_row = jnp.maximum(inter, jnp.max(Dm, axis=-1))
        w_intra = jnp.exp(Dm - m_row[..., None])
        s_inter = jnp.exp(inter - m_row)
        qk = jnp.einsum('bhld,bhsd->bhls', qc, kc) * w_intra
        num = s_inter[..., None] * jnp.einsum('bhld,bhde->bhle', qc, C) \
            + jnp.einsum('bhls,bhse->bhle', qk, vc)
        den = s_inter * jnp.einsum('bhld,bhd->bhl', qc, n) + jnp.sum(qk, axis=-1)
        h = num / jnp.maximum(jnp.abs(den), jnp.exp(-m_row))[..., None]
        b_last = b[..., -1]
        a = b_last[..., None] - b + li
        m_new = jnp.maximum(b_last + m, jnp.max(a, axis=-1))
        s_state = jnp.exp(b_last + m - m_new)
        kw = kc * jnp.exp(a - m_new[..., None])[..., None]
        C_new = s_state[..., None, None] * C + jnp.einsum('bhsd,bhse->bhde', kw, vc)
        n_new = s_state[..., None] * n + jnp.sum(kw, axis=-2)
        return (C_new, n_new, m_new), h

    init = (jnp.zeros((B, H, dqk, dv), jnp.float32),
            jnp.zeros((B, H, dqk), jnp.float32),
            jnp.zeros((B, H), jnp.float32))
    xs = (to_chunks(q), to_chunks(k), to_chunks(v), gate_chunks(log_i), gate_chunks(log_f))
    _, h = lax.scan(step, init, xs)
    h = h.transpose(1, 0, 3, 2, 4).reshape(B, S, H, dv)
    return h.astype(out_dtype)


def causal_depthwise_conv(u, w):
    S = u.shape[1]
    up = jnp.pad(u, ((0, 0), (CONV_K - 1, 0), (0, 0)))
    y = w[CONV_K - 1] * up[:, CONV_K - 1:CONV_K - 1 + S, :]
    for j in range(CONV_K - 1):
        y = y + w[j] * up[:, j:j + S, :]
    return y


def hybrid_mixer(h, w_in, b_if, mlstm_head_norm, conv_w, w_branch, w_mix_out):
    B, S, _ = h.shape
    proj = h @ w_in
    sizes = [MLSTM_QK, MLSTM_QK, MLSTM_V, MLSTM_V, MLSTM_HEADS, MLSTM_HEADS,
             CONV_WIDTH, CONV_WIDTH, CONV_WIDTH, N_BRANCH * D_MODEL]
    idx = [int(s) for s in np.cumsum(sizes)[:-1]]
    q, k, v, o_pre, i_pre, f_pre, cb, cc, cx, g_pre = jnp.split(proj, idx, axis=-1)

    q = q.reshape(B, S, MLSTM_HEADS, MLSTM_DQK)
    k = k.reshape(B, S, MLSTM_HEADS, MLSTM_DQK)
    v = v.reshape(B, S, MLSTM_HEADS, MLSTM_DV)
    i_pre = GATE_CAP * jnp.tanh((i_pre + b_if[0]) / GATE_CAP)
    f_pre = GATE_CAP * jnp.tanh((f_pre + b_if[1]) / GATE_CAP)
    hA = mlstm_chunkwise(q, k, v, i_pre, f_pre)
    hA = rms_norm(hA, mlstm_head_norm.reshape(MLSTM_HEADS, MLSTM_DV))
    hA = jax.nn.sigmoid(o_pre) * hA.reshape(B, S, MLSTM_V)

    hB = cb * causal_depthwise_conv(cc * cx, conv_w)

    ys = jnp.einsum('bsnc,ncd->bsnd', jnp.stack([hA, hB], axis=2), w_branch)
    gates = jax.nn.sigmoid(g_pre.reshape(B, S, N_BRANCH, D_MODEL))
    merged = jnp.sum(gates * ys, axis=2)
    return merged @ w_mix_out


def memory_cross_attention(h, m, w_xq, w_xk, w_xv, w_xo):
    B, S, _ = h.shape
    M = m.shape[1]
    q = (h @ w_xq).reshape(B, S, XATTN_HEADS, XATTN_DH)
    k = (m @ w_xk).reshape(B, M, XATTN_HEADS, XATTN_DH)
    v = (m @ w_xv).reshape(B, M, XATTN_HEADS, XATTN_DH)
    s = jnp.einsum('bshd,bmhd->bhsm', q.astype(jnp.float32), k.astype(jnp.float32))
    p = jax.nn.softmax(s * (XATTN_DH ** -0.5), axis=-1).astype(v.dtype)
    o = jnp.einsum('bhsm,bmhd->bshd', p, v).reshape(B, S, XATTN_HEADS * XATTN_DH)
    return o @ w_xo


def swiglu_ffn(h, w_gate, w_up, w_down):
    return (jax.nn.silu(h @ w_gate) * (h @ w_up)) @ w_down


def setup_inputs(seed: int = 0) -> dict:
    key = jax.random.key(seed)
    ks = jax.random.split(key, 24)

    def nrm(k, shape, scale):
        return jax.random.normal(k, shape, jnp.float32) * scale

    def gain(k, shape):
        return 1.0 + 0.02 * jax.random.normal(k, shape, jnp.float32)

    b_i = 0.1 * jax.random.normal(ks[4], (DEPTH, MLSTM_HEADS), jnp.float32)
    b_f = jnp.linspace(3.0, 6.0, MLSTM_HEADS, dtype=jnp.float32)[None, :] \
        + 0.1 * jax.random.normal(ks[5], (DEPTH, MLSTM_HEADS), jnp.float32)
    return {
        "x": nrm(ks[0], (BATCH, SEQ, D_MODEL), 1.0),
        "mem": nrm(ks[1], (BATCH, MEM_LEN, D_MODEL), 1.0),
        "norm_pre_mix": gain(ks[2], (DEPTH, D_MODEL)),
        "norm_post_mix": gain(ks[3], (DEPTH, D_MODEL)),
        "w_in": nrm(ks[6], (DEPTH, D_MODEL, IN_WIDTH), D_MODEL ** -0.5),
        "b_if": jnp.stack([b_i, b_f], axis=1),
        "mlstm_head_norm": gain(ks[7], (DEPTH, MLSTM_V)),
        "conv_w": nrm(ks[8], (DEPTH, CONV_K, CONV_WIDTH), CONV_K ** -0.5),
        "w_branch": nrm(ks[9], (DEPTH, N_BRANCH, MLSTM_V, D_MODEL), MLSTM_V ** -0.5),
        "w_mix_out": nrm(ks[10], (DEPTH, D_MODEL, D_MODEL), D_MODEL ** -0.5),
        "norm_pre_xattn": gain(ks[11], (DEPTH, D_MODEL)),
        "norm_post_xattn": gain(ks[12], (DEPTH, D_MODEL)),
        "norm_mem": gain(ks[13], (DEPTH, D_MODEL)),
        "w_xq": nrm(ks[14], (DEPTH, D_MODEL, XATTN_HEADS * XATTN_DH), D_MODEL ** -0.5),
        "w_xk": nrm(ks[15], (DEPTH, D_MODEL, XATTN_HEADS * XATTN_DH), D_MODEL ** -0.5),
        "w_xv": nrm(ks[16], (DEPTH, D_MODEL, XATTN_HEADS * XATTN_DH), D_MODEL ** -0.5),
        "w_xo": nrm(ks[17], (DEPTH, XATTN_HEADS * XATTN_DH, D_MODEL), (XATTN_HEADS * XATTN_DH) ** -0.5),
        "norm_pre_ffn": gain(ks[18], (DEPTH, D_MODEL)),
        "norm_post_ffn": gain(ks[19], (DEPTH, D_MODEL)),
        "w_ffn_gate": nrm(ks[20], (DEPTH, D_MODEL, D_FF), D_MODEL ** -0.5),
        "w_ffn_up": nrm(ks[21], (DEPTH, D_MODEL, D_FF), D_MODEL ** -0.5),
        "w_ffn_down": nrm(ks[22], (DEPTH, D_FF, D_MODEL), D_FF ** -0.5),
    }


def reference(x, mem, norm_pre_mix, norm_post_mix, w_in, b_if, mlstm_head_norm, conv_w,
              w_branch, w_mix_out, norm_pre_xattn, norm_post_xattn, norm_mem,
              w_xq, w_xk, w_xv, w_xo, norm_pre_ffn, norm_post_ffn,
              w_ffn_gate, w_ffn_up, w_ffn_down):
    for l in range(DEPTH):
        h = rms_norm(x, norm_pre_mix[l])
        y = hybrid_mixer(h, w_in[l], b_if[l], mlstm_head_norm[l], conv_w[l],
                         w_branch[l], w_mix_out[l])
        x = x + rms_norm(y, norm_post_mix[l])
        h = rms_norm(x, norm_pre_xattn[l])
        m = rms_norm(mem, norm_mem[l])
        y = memory_cross_attention(h, m, w_xq[l], w_xk[l], w_xv[l], w_xo[l])
        x = x + rms_norm(y, norm_post_xattn[l])
        h = rms_norm(x, norm_pre_ffn[l])
        y = swiglu_ffn(h, w_ffn_gate[l], w_ffn_up[l], w_ffn_down[l])
        x = x + rms_norm(y, norm_post_ffn[l])
    return x
```

```python
import functools

import jax
import jax.numpy as jnp
from jax import lax
from jax.experimental import pallas as pl
from jax.experimental.pallas import tpu as pltpu

F32 = jnp.float32
BF16 = jnp.bfloat16

MLSTM_HEADS = 4
XATTN_HEADS = 4
CONV_K = 3
GATE_CAP = 15.0
EPS = 1e-6
MLSTM_CHUNK = 256
GATE_LANES = 128

VMEM_LIMIT_BYTES = 56 * 1024 * 1024


def _params(*sem):
    return pltpu.CompilerParams(dimension_semantics=sem,
                                vmem_limit_bytes=VMEM_LIMIT_BYTES)


def _rms(x, w):
    return x * lax.rsqrt(jnp.mean(x * x, axis=-1, keepdims=True) + EPS) * w


def _prenorm_kernel(x_ref, w_ref, h_ref):
    h_ref[...] = _rms(x_ref[...].astype(F32), w_ref[...]).astype(h_ref.dtype)


def prenorm(x, w, tr=256):
    M, D = x.shape
    return pl.pallas_call(
        _prenorm_kernel,
        grid=(M // tr,),
        in_specs=[pl.BlockSpec((tr, D), lambda i: (i, 0)),
                  pl.BlockSpec((1, D), lambda i: (0, 0))],
        out_specs=pl.BlockSpec((tr, D), lambda i: (i, 0)),
        out_shape=jax.ShapeDtypeStruct((M, D), BF16),
        compiler_params=_params("parallel"),
        name="prenorm",
    )(x, w.reshape(1, D))


def _post_pre_kernel(y_ref, x_ref, wpost_ref, wpre_ref, xo_ref, h_ref):
    xn = x_ref[...] + _rms(y_ref[...].astype(F32), wpost_ref[...])
    xo_ref[...] = xn
    h_ref[...] = _rms(xn, wpre_ref[...]).astype(h_ref.dtype)


def _post_kernel(y_ref, x_ref, wpost_ref, xo_ref):
    xo_ref[...] = x_ref[...] + _rms(y_ref[...].astype(F32), wpost_ref[...])


def post_norm_residual(y, x, w_post, w_pre_next=None, tr=256):
    M, D = x.shape
    row = pl.BlockSpec((tr, D), lambda i: (i, 0))
    vec = pl.BlockSpec((1, D), lambda i: (0, 0))
    if w_pre_next is None:
        return pl.pallas_call(
            _post_kernel, grid=(M // tr,),
            in_specs=[row, row, vec], out_specs=row,
            out_shape=jax.ShapeDtypeStruct((M, D), F32),
            compiler_params=_params("parallel"), name="post_norm",
        )(y, x, w_post.reshape(1, D))
    return pl.pallas_call(
        _post_pre_kernel, grid=(M // tr,),
        in_specs=[row, row, vec, vec], out_specs=[row, row],
        out_shape=[jax.ShapeDtypeStruct((M, D), F32),
                   jax.ShapeDtypeStruct((M, D), BF16)],
        compiler_params=_params("parallel"), name="post_pre_norm",
    )(y, x, w_post.reshape(1, D), w_pre_next.reshape(1, D))


def _mm_kernel(a_ref, w_ref, o_ref):
    o_ref[...] = jnp.dot(a_ref[...], w_ref[...],
                         preferred_element_type=F32).astype(o_ref.dtype)


def matmul(a, w, out_dtype, tm, tn):
    M, K = a.shape
    N = w.shape[1]
    return pl.pallas_call(
        _mm_kernel,
        grid=(N // tn, M // tm),
        in_specs=[pl.BlockSpec((tm, K), lambda j, i: (i, 0)),
                  pl.BlockSpec((K, tn), lambda j, i: (0, j))],
        out_specs=pl.BlockSpec((tm, tn), lambda j, i: (i, j)),
        out_shape=jax.ShapeDtypeStruct((M, N), out_dtype),
        compiler_params=_params("parallel", "parallel"),
        name="matmul",
    )(a, w)


def _mm_ksplit_kernel(a_ref, w_ref, o_ref):
    d = jnp.dot(a_ref[...], w_ref[...], preferred_element_type=F32)
    k = pl.program_id(2)

    @pl.when(k == 0)
    def _():
        o_ref[...] = d

    @pl.when(k > 0)
    def _():
        o_ref[...] += d


def matmul_ksplit(a, w, tm, tn, tk):
    M, K = a.shape
    N = w.shape[1]
    return pl.pallas_call(
        _mm_ksplit_kernel,
        grid=(N // tn, M // tm, K // tk),
        in_specs=[pl.BlockSpec((tm, tk), lambda j, i, k: (i, k)),
                  pl.BlockSpec((tk, tn), lambda j, i, k: (k, j))],
        out_specs=pl.BlockSpec((tm, tn), lambda j, i, k: (i, j)),
        out_shape=jax.ShapeDtypeStruct((M, N), F32),
        compiler_params=_params("parallel", "parallel", "arbitrary"),
        name="matmul_ksplit",
    )(a, w)


def _swiglu_kernel(a_ref, wg_ref, wu_ref, o_ref):
    a = a_ref[...]
    g = jnp.dot(a, wg_ref[...], preferred_element_type=F32)
    u = jnp.dot(a, wu_ref[...], preferred_element_type=F32)
    o_ref[...] = (g * jax.nn.sigmoid(g) * u).astype(o_ref.dtype)


def swiglu_up(a, wg, wu, tm, tn):
    M, K = a.shape
    N = wg.shape[1]
    wspec = pl.BlockSpec((K, tn), lambda i, j: (0, j))
    return pl.pallas_call(
        _swiglu_kernel,
        grid=(M // tm, N // tn),
        in_specs=[pl.BlockSpec((tm, K), lambda i, j: (i, 0)), wspec, wspec],
        out_specs=pl.BlockSpec((tm, tn), lambda i, j: (i, j)),
        out_shape=jax.ShapeDtypeStruct((M, N), BF16),
        compiler_params=_params("parallel", "parallel"),
        name="swiglu_up",
    )(a, wg, wu)


def _branch_kernel(a0_ref, a1_ref, w0_ref, w1_ref, g0_ref, g1_ref, o_ref):
    y0 = jnp.dot(a0_ref[...], w0_ref[...], preferred_element_type=F32)
    y1 = jnp.dot(a1_ref[...], w1_ref[...], preferred_element_type=F32)
    o_ref[...] = (jax.nn.sigmoid(g0_ref[...].astype(F32)) * y0
                  + jax.nn.sigmoid(g1_ref[...].astype(F32)) * y1).astype(o_ref.dtype)


def branch_merge(a0, a1, w0, w1, proj, g_col0, tm, tn):
    M, K = a0.shape
    N = w0.shape[1]
    gb0 = g_col0 // tn
    gb1 = (g_col0 + N) // tn
    aspec = pl.BlockSpec((tm, K), lambda j, i: (i, 0))
    wspec = pl.BlockSpec((K, tn), lambda j, i: (0, j))
    return pl.pallas_call(
        _branch_kernel,
        grid=(N // tn, M // tm),
        in_specs=[aspec, aspec, wspec, wspec,
                  pl.BlockSpec((tm, tn), lambda j, i: (i, gb0 + j)),
                  pl.BlockSpec((tm, tn), lambda j, i: (i, gb1 + j))],
        out_specs=pl.BlockSpec((tm, tn), lambda j, i: (i, j)),
        out_shape=jax.ShapeDtypeStruct((M, N), BF16),
        compiler_params=_params("parallel", "parallel"),
        name="branch_merge",
    )(a0, a1, w0, w1, proj, proj)


def _mlstm_kernel(bif_ref, q_ref, k_ref, v_ref, o_ref, gt_ref, hn_ref, out_ref,
                  c_ref, n_ref, m_ref):
    h = pl.program_id(1)
    t = pl.program_id(2)
    L, dqk = q_ref.shape
    scale = dqk ** -0.5

    @pl.when(t == 0)
    def _():
        c_ref[...] = jnp.zeros_like(c_ref)
        n_ref[...] = jnp.zeros_like(n_ref)
        m_ref[...] = jnp.zeros_like(m_ref)

    i_raw = gt_ref[pl.ds(h, 1), :]
    f_raw = gt_ref[pl.ds(MLSTM_HEADS + h, 1), :]
    li = GATE_CAP * jnp.tanh((i_raw + bif_ref[0, h]) / GATE_CAP)
    fp = GATE_CAP * jnp.tanh((f_raw + bif_ref[1, h]) / GATE_CAP)
    lf = jnp.minimum(fp, 0.0) - jnp.log1p(jnp.exp(-jnp.abs(fp)))

    lane = lax.broadcasted_iota(jnp.int32, (8, L), 1)
    b8 = jnp.broadcast_to(lf, (8, L))
    sh = 1
    while sh < L:
        b8 = b8 + jnp.where(lane >= sh, pltpu.roll(b8, sh, axis=1), 0.0)
        sh *= 2
    b_row = b8[0:1, :]
    m_prev = m_ref[:, 0:1]

    b_sq = jnp.broadcast_to(b_row, (L, L)).T
    row = lax.broadcasted_iota(jnp.int32, (L, L), 0)
    col = lax.broadcasted_iota(jnp.int32, (L, L), 1)
    dm = jnp.where(col <= row, b_sq - b_row + li, -jnp.inf)
    inter = b_sq[:, 0:1] + m_prev
    m_row = jnp.maximum(inter, jnp.max(dm, axis=-1, keepdims=True))
    w_intra = jnp.exp(dm - m_row)
    s_inter = jnp.exp(inter - m_row)

    q = q_ref[...]
    k = k_ref[...]
    v = v_ref[...]
    c_old = c_ref[...]
    n_old = n_ref[...]
    qk = lax.dot_general(q, k, (((1,), (1,)), ((), ())),
                         preferred_element_type=F32) * scale * w_intra
    q_c = jnp.dot(q, c_old.astype(BF16), preferred_element_type=F32) * scale
    num = s_inter * q_c + jnp.dot(qk.astype(BF16), v, preferred_element_type=F32)
    q_n = jnp.sum(q.astype(F32) * n_old, axis=-1, keepdims=True) * scale
    den = s_inter * q_n + jnp.sum(qk, axis=-1, keepdims=True)
    hh = num / jnp.maximum(jnp.abs(den), jnp.exp(-m_row))

    b_last = b_row[:, L - 1:L]
    a_row = b_last - b_row + li
    m_new = jnp.maximum(b_last + m_prev, jnp.max(a_row, axis=-1, keepdims=True))
    s_state = jnp.exp(b_last + m_prev - m_new)
    e_col = jnp.broadcast_to(jnp.exp(a_row - m_new), (L, L)).T[:, 0:1]
    kw = k.astype(F32) * e_col
    c_ref[...] = s_state * c_old + lax.dot_general(
        kw.astype(BF16), v, (((0,), (0,)), ((), ())), preferred_element_type=F32)
    n_ref[...] = s_state * n_old + jnp.sum(kw, axis=0, keepdims=True)
    m_ref[...] = jnp.broadcast_to(m_new, m_ref.shape)

    out_ref[...] = (jax.nn.sigmoid(o_ref[...].astype(F32))
                    * _rms(hh, hn_ref[...])).astype(out_ref.dtype)


def mlstm_mixer(proj, gates_t, b_if, head_norm, batch, seq, q_col0, k_col0, v_col0, o_col0,
                dqk, dv):
    L = MLSTM_CHUNK
    T = seq // L
    H = MLSTM_HEADS
    qb, kb, vb, ob = q_col0 // dqk, k_col0 // dqk, v_col0 // dv, o_col0 // dv
    return pl.pallas_call(
        _mlstm_kernel,
        grid=(batch, H, T),
        in_specs=[
            pl.BlockSpec(memory_space=pltpu.SMEM),
            pl.BlockSpec((L, dqk), lambda b, h, t: (b * T + t, qb + h)),
            pl.BlockSpec((L, dqk), lambda b, h, t: (b * T + t, kb + h)),
            pl.BlockSpec((L, dv), lambda b, h, t: (b * T + t, vb + h)),
            pl.BlockSpec((L, dv), lambda b, h, t: (b * T + t, ob + h)),
            pl.BlockSpec((2 * H, L), lambda b, h, t: (0, b * T + t)),
            pl.BlockSpec((1, dv), lambda b, h, t: (0, h)),
        ],
        out_specs=pl.BlockSpec((L, dv), lambda b, h, t: (b * T + t, h)),
        out_shape=jax.ShapeDtypeStruct((batch * seq, H * dv), BF16),
        scratch_shapes=[pltpu.VMEM((dqk, dv), F32),
                        pltpu.VMEM((1, dqk), F32),
                        pltpu.VMEM((1, 128), F32)],
        compiler_params=_params("parallel", "parallel", "arbitrary"),
        name="mlstm",
    )(b_if, proj, proj, proj, proj, gates_t, head_norm.reshape(1, H * dv))


def _conv_kernel(cb_ref, cc_ref, cx_ref, w_ref, o_ref, carry_ref):
    s = pl.program_id(2)
    ts = cb_ref.shape[0]

    @pl.when(s == 0)
    def _():
        carry_ref[...] = jnp.zeros_like(carry_ref)

    w = w_ref[...]
    w0, w1, w2 = w[0:1, :], w[1:2, :], w[2:3, :]
    u = cc_ref[...].astype(F32) * cx_ref[...].astype(F32)
    y = w2 * u + w1 * pltpu.roll(u, 1, axis=0) + w0 * pltpu.roll(u, 2, axis=0)
    o_ref[...] = (cb_ref[...].astype(F32) * y).astype(o_ref.dtype)

    prev = carry_ref[...]
    u_h = u[0:8, :]
    r = lax.broadcasted_iota(jnp.int32, u_h.shape, 0)
    u1 = jnp.where(r < 1, pltpu.roll(prev, 1, axis=0), pltpu.roll(u_h, 1, axis=0))
    u2 = jnp.where(r < 2, pltpu.roll(prev, 2, axis=0), pltpu.roll(u_h, 2, axis=0))
    y_h = w2 * u_h + w1 * u1 + w0 * u2
    o_ref[0:8, :] = (cb_ref[0:8, :].astype(F32) * y_h).astype(o_ref.dtype)
    carry_ref[...] = u[ts - 8:ts, :]


def gated_conv(proj, conv_w, batch, seq, cb_col0, cc_col0, cx_col0, ts=512, tc=512):
    C = conv_w.shape[1]
    nS = seq // ts
    b0, c0, x0 = cb_col0 // tc, cc_col0 // tc, cx_col0 // tc
    return pl.pallas_call(
        _conv_kernel,
        grid=(batch, C // tc, nS),
        in_specs=[
            pl.BlockSpec((ts, tc), lambda b, c, s: (b * nS + s, b0 + c)),
            pl.BlockSpec((ts, tc), lambda b, c, s: (b * nS + s, c0 + c)),
            pl.BlockSpec((ts, tc), lambda b, c, s: (b * nS + s, x0 + c)),
            pl.BlockSpec((CONV_K, tc), lambda b, c, s: (0, c)),
        ],
        out_specs=pl.BlockSpec((ts, tc), lambda b, c, s: (b * nS + s, c)),
        out_shape=jax.ShapeDtypeStruct((batch * seq, C), BF16),
        scratch_shapes=[pltpu.VMEM((8, tc), F32)],
        compiler_params=_params("parallel", "parallel", "arbitrary"),
        name="gated_conv",
    )(proj, proj, proj, conv_w)


def _xattn_kernel(q_ref, k_ref, v_ref, o_ref):
    dh = q_ref.shape[1]
    s = lax.dot_general(q_ref[...], k_ref[...], (((1,), (1,)), ((), ())),
                        preferred_element_type=F32) * (dh ** -0.5)
    e = jnp.exp(s - jnp.max(s, axis=-1, keepdims=True))
    p = e / jnp.sum(e, axis=-1, keepdims=True)
    o_ref[...] = jnp.dot(p.astype(BF16), v_ref[...],
                         preferred_element_type=F32).astype(o_ref.dtype)


def cross_attention(q, kv, batch, seq, mem_len, dh, ts=512):
    H = XATTN_HEADS
    nS = seq // ts
    return pl.pallas_call(
        _xattn_kernel,
        grid=(batch, H, nS),
        in_specs=[
            pl.BlockSpec((ts, dh), lambda b, h, s: (b * nS + s, h)),
            pl.BlockSpec((mem_len, dh), lambda b, h, s: (b, h)),
            pl.BlockSpec((mem_len, dh), lambda b, h, s: (b, H + h)),
        ],
        out_specs=pl.BlockSpec((ts, dh), lambda b, h, s: (b * nS + s, h)),
        out_shape=jax.ShapeDtypeStruct((batch * seq, H * dh), BF16),
        compiler_params=_params("parallel", "parallel", "parallel"),
        name="cross_attention",
    )(q, kv, kv)


def kernel(x, mem, norm_pre_mix, norm_post_mix, w_in, b_if, mlstm_head_norm, conv_w, w_branch, w_mix_out, norm_pre_xattn, norm_post_xattn, norm_mem, w_xq, w_xk, w_xv, w_xo, norm_pre_ffn, norm_post_ffn, w_ffn_gate, w_ffn_up, w_ffn_down):
    B, S, D = x.shape
    mem_len = mem.shape[1]
    depth = w_in.shape[0]
    H = MLSTM_HEADS
    v_width = mlstm_head_norm.shape[1]
    conv_width = conv_w.shape[2]
    dv = v_width // H
    dqk = dv // 2
    qk_width = H * dqk
    dh = w_xq.shape[2] // XATTN_HEADS
    d_ff = w_ffn_gate.shape[2]

    q0 = 0
    k0 = q0 + qk_width
    v0 = k0 + qk_width
    o0 = v0 + v_width
    gate0 = o0 + v_width
    cb0 = gate0
    cc0 = cb0 + conv_width
    cx0 = cc0 + conv_width
    g0 = cx0 + conv_width

    xr = x.reshape(B * S, D)
    memr = mem.reshape(B * mem_len, D)
    for l in range(depth):
        w_main = jnp.concatenate([w_in[l][:, :gate0], w_in[l][:, gate0 + 2 * H:]],
                                 axis=1).astype(BF16)
        w_gates = jnp.pad(w_in[l][:, gate0:gate0 + 2 * H],
                          ((0, 0), (0, GATE_LANES - 2 * H))).astype(BF16)

        h = prenorm(xr, norm_pre_mix[l])
        proj = matmul(h, w_main, BF16, 1024, 1024)
        gates = matmul(h, w_gates, F32, 1024, GATE_LANES)
        gates_t = gates[:, :2 * H].T
        h_a = mlstm_mixer(proj, gates_t, b_if[l], mlstm_head_norm[l], B, S,
                          q0, k0, v0, o0, dqk, dv)
        h_b = gated_conv(proj, conv_w[l], B, S, cb0, cc0, cx0)
        merged = branch_merge(h_a, h_b, w_branch[l, 0].astype(BF16),
                              w_branch[l, 1].astype(BF16), proj, g0, 1024, 512)
        y = matmul(merged, w_mix_out[l].astype(BF16), F32, 1024, 1024)
        xr, h = post_norm_residual(y, xr, norm_post_mix[l], norm_pre_xattn[l])

        m = prenorm(memr, norm_mem[l])
        w_kv = jnp.concatenate([w_xk[l], w_xv[l]], axis=1).astype(BF16)
        kv = matmul(m, w_kv, BF16, B * mem_len, 1024)
        q = matmul(h, w_xq[l].astype(BF16), BF16, 1024, 1024)
        o = cross_attention(q, kv, B, S, mem_len, dh)
        y = matmul(o, w_xo[l].astype(BF16), F32, 1024, 1024)
        xr, h = post_norm_residual(y, xr, norm_post_xattn[l], norm_pre_ffn[l])

        hidden = swiglu_up(h, w_ffn_gate[l].astype(BF16), w_ffn_up[l].astype(BF16),
                           2048, 256)
        y = matmul_ksplit(hidden, w_ffn_down[l].astype(BF16), 1024, 512, d_ff // 2)
        xr = post_norm_residual(y, xr, norm_post_ffn[l])
    return xr.reshape(B, S, D)
```

```python
import functools

import jax
import jax.numpy as jnp
from jax import lax
from jax.experimental import pallas as pl
from jax.experimental.pallas import tpu as pltpu

F32 = jnp.float32
BF16 = jnp.bfloat16

MLSTM_HEADS = 4
XATTN_HEADS = 4
CONV_K = 3
GATE_CAP = 15.0
EPS = 1e-6
MLSTM_CHUNK = 256
GATE_LANES = 128

VMEM_LIMIT_BYTES = 56 * 1024 * 1024


def _params(*sem):
    return pltpu.CompilerParams(dimension_semantics=sem,
                                vmem_limit_bytes=VMEM_LIMIT_BYTES)


def _rms(x, w):
    return x * lax.rsqrt(jnp.mean(x * x, axis=-1, keepdims=True) + EPS) * w


def _prenorm_kernel(x_ref, w_ref, h_ref):
    h_ref[...] = _rms(x_ref[...].astype(F32), w_ref[...]).astype(h_ref.dtype)


def prenorm(x, w, tr=256):
    M, D = x.shape
    return pl.pallas_call(
        _prenorm_kernel,
        grid=(M // tr,),
        in_specs=[pl.BlockSpec((tr, D), lambda i: (i, 0)),
                  pl.BlockSpec((1, D), lambda i: (0, 0))],
        out_specs=pl.BlockSpec((tr, D), lambda i: (i, 0)),
        out_shape=jax.ShapeDtypeStruct((M, D), BF16),
        compiler_params=_params("parallel"),
        name="prenorm",
    )(x, w.reshape(1, D))


def _post_pre_kernel(y_ref, x_ref, wpost_ref, wpre_ref, xo_ref, h_ref):
    xn = x_ref[...] + _rms(y_ref[...].astype(F32), wpost_ref[...])
    xo_ref[...] = xn
    h_ref[...] = _rms(xn, wpre_ref[...]).astype(h_ref.dtype)


def _post_kernel(y_ref, x_ref, wpost_ref, xo_ref):
    xo_ref[...] = x_ref[...] + _rms(y_ref[...].astype(F32), wpost_ref[...])


def post_norm_residual(y, x, w_post, w_pre_next=None, tr=256):
    M, D = x.shape
    row = pl.BlockSpec((tr, D), lambda i: (i, 0))
    vec = pl.BlockSpec((1, D), lambda i: (0, 0))
    if w_pre_next is None:
        return pl.pallas_call(
            _post_kernel, grid=(M // tr,),
            in_specs=[row, row, vec], out_specs=row,
            out_shape=jax.ShapeDtypeStruct((M, D), F32),
            compiler_params=_params("parallel"), name="post_norm",
        )(y, x, w_post.reshape(1, D))
    return pl.pallas_call(
        _post_pre_kernel, grid=(M // tr,),
        in_specs=[row, row, vec, vec], out_specs=[row, row],
        out_shape=[jax.ShapeDtypeStruct((M, D), F32),
                   jax.ShapeDtypeStruct((M, D), BF16)],
        compiler_params=_params("parallel"), name="post_pre_norm",
    )(y, x, w_post.reshape(1, D), w_pre_next.reshape(1, D))


def _weight_spec(w, lead, k_rows, tn, index_map):
    assert w.ndim == len(lead) + 2
    return pl.BlockSpec((None,) * len(lead) + (k_rows, tn),
                        lambda *g: tuple(lead) + index_map(*g))


def _mm_kernel(a_ref, w_ref, o_ref):
    o_ref[...] = jnp.dot(a_ref[...], w_ref[...].astype(BF16),
                         preferred_element_type=F32).astype(o_ref.dtype)


def matmul(a, w, out_dtype, tm, tn, lead=(), n_cols=None):
    M, K = a.shape
    N = w.shape[-1] if n_cols is None else n_cols
    assert w.shape[-2] == K and M % tm == 0 and N % tn == 0
    return pl.pallas_call(
        _mm_kernel,
        grid=(N // tn, M // tm),
        in_specs=[pl.BlockSpec((tm, K), lambda j, i: (i, 0)),
                  _weight_spec(w, lead, K, tn, lambda j, i: (0, j))],
        out_specs=pl.BlockSpec((tm, tn), lambda j, i: (i, j)),
        out_shape=jax.ShapeDtypeStruct((M, N), out_dtype),
        compiler_params=_params("parallel", "parallel"),
        name="matmul",
    )(a, w)


def _mm_ksplit_kernel(a_ref, w_ref, o_ref, acc_ref):
    d = jnp.dot(a_ref[...], w_ref[...], preferred_element_type=F32)
    k = pl.program_id(2)

    @pl.when(k == 0)
    def _():
        acc_ref[...] = d

    last = pl.num_programs(2) - 1

    @pl.when((k > 0) & (k < last))
    def _():
        acc_ref[...] += d

    @pl.when(k == last)
    def _():
        o_ref[...] = (acc_ref[...] + d).astype(o_ref.dtype)


def matmul_ksplit(a, w, out_dtype, tm, tn, tk):
    M, K = a.shape
    N = w.shape[1]
    assert M % tm == 0 and N % tn == 0 and K % tk == 0 and K // tk >= 2
    return pl.pallas_call(
        _mm_ksplit_kernel,
        grid=(N // tn, M // tm, K // tk),
        in_specs=[pl.BlockSpec((tm, tk), lambda j, i, k: (i, k)),
                  pl.BlockSpec((tk, tn), lambda j, i, k: (k, j))],
        out_specs=pl.BlockSpec((tm, tn), lambda j, i, k: (i, j)),
        out_shape=jax.ShapeDtypeStruct((M, N), out_dtype),
        scratch_shapes=[pltpu.VMEM((tm, tn), F32)],
        compiler_params=_params("parallel", "parallel", "arbitrary"),
        name="matmul_ksplit",
    )(a, w)


def _swiglu_kernel(a_ref, wg_ref, wu_ref, o_ref):
    a = a_ref[...]
    g = jnp.dot(a, wg_ref[...].astype(BF16), preferred_element_type=F32)
    u = jnp.dot(a, wu_ref[...].astype(BF16), preferred_element_type=F32)
    o_ref[...] = (g * jax.nn.sigmoid(g) * u).astype(o_ref.dtype)


def swiglu_up(a, wg, wu, tm, tn, lead=()):
    M, K = a.shape
    N = wg.shape[-1]
    assert M % tm == 0 and N % tn == 0
    wspec = _weight_spec(wg, lead, K, tn, lambda i, j: (0, j))
    return pl.pallas_call(
        _swiglu_kernel,
        grid=(M // tm, N // tn),
        in_specs=[pl.BlockSpec((tm, K), lambda i, j: (i, 0)), wspec, wspec],
        out_specs=pl.BlockSpec((tm, tn), lambda i, j: (i, j)),
        out_shape=jax.ShapeDtypeStruct((M, N), BF16),
        compiler_params=_params("parallel", "parallel"),
        name="swiglu_up",
    )(a, wg, wu)


def _branch_kernel(a0_ref, a1_ref, w0_ref, w1_ref, g0_ref, g1_ref, o_ref):
    y0 = jnp.dot(a0_ref[...], w0_ref[...].astype(BF16), preferred_element_type=F32)
    y1 = jnp.dot(a1_ref[...], w1_ref[...].astype(BF16), preferred_element_type=F32)
    o_ref[...] = (jax.nn.sigmoid(g0_ref[...].astype(F32)) * y0
                  + jax.nn.sigmoid(g1_ref[...].astype(F32)) * y1).astype(o_ref.dtype)


def branch_merge(a0, a1, w, lead, proj, g_col0, tm, tn):
    M, K = a0.shape
    N = w.shape[-1]
    assert M % tm == 0 and N % tn == 0 and g_col0 % tn == 0
    gb0 = g_col0 // tn
    gb1 = (g_col0 + N) // tn
    aspec = pl.BlockSpec((tm, K), lambda j, i: (i, 0))
    return pl.pallas_call(
        _branch_kernel,
        grid=(N // tn, M // tm),
        in_specs=[aspec, aspec,
                  _weight_spec(w, tuple(lead) + (0,), K, tn, lambda j, i: (0, j)),
                  _weight_spec(w, tuple(lead) + (1,), K, tn, lambda j, i: (0, j)),
                  pl.BlockSpec((tm, tn), lambda j, i: (i, gb0 + j)),
                  pl.BlockSpec((tm, tn), lambda j, i: (i, gb1 + j))],
        out_specs=pl.BlockSpec((tm, tn), lambda j, i: (i, j)),
        out_shape=jax.ShapeDtypeStruct((M, N), BF16),
        compiler_params=_params("parallel", "parallel"),
        name="branch_merge",
    )(a0, a1, w, w, proj, proj)


def _mlstm_kernel(bif_ref, q_ref, k_ref, v_ref, o_ref, gt_ref, hn_ref, out_ref,
                  c_ref, n_ref, m_ref):
    h = pl.program_id(1)
    t = pl.program_id(2)
    L, dqk = q_ref.shape
    scale = dqk ** -0.5

    @pl.when(t == 0)
    def _():
        c_ref[...] = jnp.zeros_like(c_ref)
        n_ref[...] = jnp.zeros_like(n_ref)
        m_ref[...] = jnp.zeros_like(m_ref)

    i_raw = gt_ref[pl.ds(h, 1), :]
    f_raw = gt_ref[pl.ds(MLSTM_HEADS + h, 1), :]
    li = GATE_CAP * jnp.tanh((i_raw + bif_ref[0, h]) / GATE_CAP)
    fp = GATE_CAP * jnp.tanh((f_raw + bif_ref[1, h]) / GATE_CAP)
    lf = jnp.minimum(fp, 0.0) - jnp.log1p(jnp.exp(-jnp.abs(fp)))

    lane = lax.broadcasted_iota(jnp.int32, (8, L), 1)
    b8 = jnp.broadcast_to(lf, (8, L))
    sh = 1
    while sh < L:
        b8 = b8 + jnp.where(lane >= sh, pltpu.roll(b8, sh, axis=1), 0.0)
        sh *= 2
    b_row = b8[0:1, :]
    m_prev = m_ref[:, 0:1]

    b_sq = jnp.broadcast_to(b_row, (L, L)).T
    row = lax.broadcasted_iota(jnp.int32, (L, L), 0)
    col = lax.broadcasted_iota(jnp.int32, (L, L), 1)
    dm = jnp.where(col <= row, b_sq - b_row + li, -jnp.inf)
    inter = b_sq[:, 0:1] + m_prev
    m_row = jnp.maximum(inter, jnp.max(dm, axis=-1, keepdims=True))
    w_intra = jnp.exp(dm - m_row)
    s_inter = jnp.exp(inter - m_row)

    q = q_ref[...]
    k = k_ref[...]
    v = v_ref[...]
    c_old = c_ref[...]
    n_old = n_ref[...]
    qk = lax.dot_general(q, k, (((1,), (1,)), ((), ())),
                         preferred_element_type=F32) * scale * w_intra
    q_c = jnp.dot(q, c_old.astype(BF16), preferred_element_type=F32) * scale
    num = s_inter * q_c + jnp.dot(qk.astype(BF16), v, preferred_element_type=F32)
    q_n = jnp.sum(q.astype(F32) * n_old, axis=-1, keepdims=True) * scale
    den = s_inter * q_n + jnp.sum(qk, axis=-1, keepdims=True)
    hh = num / jnp.maximum(jnp.abs(den), jnp.exp(-m_row))

    b_last = b_row[:, L - 1:L]
    a_row = b_last - b_row + li
    m_new = jnp.maximum(b_last + m_prev, jnp.max(a_row, axis=-1, keepdims=True))
    s_state = jnp.exp(b_last + m_prev - m_new)
    e_col = jnp.broadcast_to(jnp.exp(a_row - m_new), (L, L)).T[:, 0:1]
    kw = k.astype(F32) * e_col
    c_ref[...] = s_state * c_old + lax.dot_general(
        kw.astype(BF16), v, (((0,), (0,)), ((), ())), preferred_element_type=F32)
    n_ref[...] = s_state * n_old + jnp.sum(kw, axis=0, keepdims=True)
    m_ref[...] = jnp.broadcast_to(m_new, m_ref.shape)

    out_ref[...] = (jax.nn.sigmoid(o_ref[...].astype(F32))
                    * _rms(hh, hn_ref[...])).astype(out_ref.dtype)


def mlstm_mixer(proj, gates_t, b_if, head_norm, batch, seq, q_col0, k_col0, v_col0, o_col0,
                dqk, dv):
    L = MLSTM_CHUNK
    T = seq // L
    H = MLSTM_HEADS
    qb, kb, vb, ob = q_col0 // dqk, k_col0 // dqk, v_col0 // dv, o_col0 // dv
    return pl.pallas_call(
        _mlstm_kernel,
        grid=(batch, H, T),
        in_specs=[
            pl.BlockSpec(memory_space=pltpu.SMEM),
            pl.BlockSpec((L, dqk), lambda b, h, t: (b * T + t, qb + h)),
            pl.BlockSpec((L, dqk), lambda b, h, t: (b * T + t, kb + h)),
            pl.BlockSpec((L, dv), lambda b, h, t: (b * T + t, vb + h)),
            pl.BlockSpec((L, dv), lambda b, h, t: (b * T + t, ob + h)),
            pl.BlockSpec((2 * H, L), lambda b, h, t: (0, b * T + t)),
            pl.BlockSpec((1, dv), lambda b, h, t: (0, h)),
        ],
        out_specs=pl.BlockSpec((L, dv), lambda b, h, t: (b * T + t, h)),
        out_shape=jax.ShapeDtypeStruct((batch * seq, H * dv), BF16),
        scratch_shapes=[pltpu.VMEM((dqk, dv), F32),
                        pltpu.VMEM((1, dqk), F32),
                        pltpu.VMEM((1, 128), F32)],
        compiler_params=_params("parallel", "parallel", "arbitrary"),
        name="mlstm",
    )(b_if, proj, proj, proj, proj, gates_t, head_norm.reshape(1, H * dv))


def _conv_kernel(cb_ref, cc_ref, cx_ref, w_ref, o_ref, carry_ref):
    s = pl.program_id(2)
    ts = cb_ref.shape[0]

    @pl.when(s == 0)
    def _():
        carry_ref[...] = jnp.zeros_like(carry_ref)

    w = w_ref[...]
    w0, w1, w2 = w[0:1, :], w[1:2, :], w[2:3, :]
    u = cc_ref[...].astype(F32) * cx_ref[...].astype(F32)
    y = w2 * u + w1 * pltpu.roll(u, 1, axis=0) + w0 * pltpu.roll(u, 2, axis=0)
    o_ref[...] = (cb_ref[...].astype(F32) * y).astype(o_ref.dtype)

    prev = carry_ref[...]
    u_h = u[0:8, :]
    r = lax.broadcasted_iota(jnp.int32, u_h.shape, 0)
    u1 = jnp.where(r < 1, pltpu.roll(prev, 1, axis=0), pltpu.roll(u_h, 1, axis=0))
    u2 = jnp.where(r < 2, pltpu.roll(prev, 2, axis=0), pltpu.roll(u_h, 2, axis=0))
    y_h = w2 * u_h + w1 * u1 + w0 * u2
    o_ref[0:8, :] = (cb_ref[0:8, :].astype(F32) * y_h).astype(o_ref.dtype)
    carry_ref[...] = u[ts - 8:ts, :]


def gated_conv(proj, conv_w, batch, seq, cb_col0, cc_col0, cx_col0, ts=512, tc=512):
    C = conv_w.shape[1]
    nS = seq // ts
    b0, c0, x0 = cb_col0 // tc, cc_col0 // tc, cx_col0 // tc
    return pl.pallas_call(
        _conv_kernel,
        grid=(batch, C // tc, nS),
        in_specs=[
            pl.BlockSpec((ts, tc), lambda b, c, s: (b * nS + s, b0 + c)),
            pl.BlockSpec((ts, tc), lambda b, c, s: (b * nS + s, c0 + c)),
            pl.BlockSpec((ts, tc), lambda b, c, s: (b * nS + s, x0 + c)),
            pl.BlockSpec((CONV_K, tc), lambda b, c, s: (0, c)),
        ],
        out_specs=pl.BlockSpec((ts, tc), lambda b, c, s: (b * nS + s, c)),
        out_shape=jax.ShapeDtypeStruct((batch * seq, C), BF16),
        scratch_shapes=[pltpu.VMEM((8, tc), F32)],
        compiler_params=_params("parallel", "parallel", "arbitrary"),
        name="gated_conv",
    )(proj, proj, proj, conv_w)


def _xattn_kernel(q_ref, k_ref, v_ref, o_ref):
    dh = q_ref.shape[1]
    s = lax.dot_general(q_ref[...], k_ref[...], (((1,), (1,)), ((), ())),
                        preferred_element_type=F32) * (dh ** -0.5)
    e = jnp.exp(s - jnp.max(s, axis=-1, keepdims=True))
    p = e / jnp.sum(e, axis=-1, keepdims=True)
    o_ref[...] = jnp.dot(p.astype(BF16), v_ref[...],
                         preferred_element_type=F32).astype(o_ref.dtype)


def cross_attention(q, k, v, batch, seq, mem_len, dh, ts=512):
    H = XATTN_HEADS
    nS = seq // ts
    return pl.pallas_call(
        _xattn_kernel,
        grid=(batch, H, nS),
        in_specs=[
            pl.BlockSpec((ts, dh), lambda b, h, s: (b * nS + s, h)),
            pl.BlockSpec((mem_len, dh), lambda b, h, s: (b, h)),
            pl.BlockSpec((mem_len, dh), lambda b, h, s: (b, h)),
        ],
        out_specs=pl.BlockSpec((ts, dh), lambda b, h, s: (b * nS + s, h)),
        out_shape=jax.ShapeDtypeStruct((batch * seq, H * dh), BF16),
        compiler_params=_params("parallel", "parallel", "parallel"),
        name="cross_attention",
    )(q, k, v)


def kernel(x, mem, norm_pre_mix, norm_post_mix, w_in, b_if, mlstm_head_norm, conv_w, w_branch, w_mix_out, norm_pre_xattn, norm_post_xattn, norm_mem, w_xq, w_xk, w_xv, w_xo, norm_pre_ffn, norm_post_ffn, w_ffn_gate, w_ffn_up, w_ffn_down):
    B, S, D = x.shape
    mem_len = mem.shape[1]
    depth = w_in.shape[0]
    H = MLSTM_HEADS
    v_width = mlstm_head_norm.shape[1]
    conv_width = conv_w.shape[2]
    dv = v_width // H
    dqk = dv // 2
    qk_width = H * dqk
    dh = w_xq.shape[2] // XATTN_HEADS
    d_ff = w_ffn_gate.shape[2]

    q0 = 0
    k0 = q0 + qk_width
    v0 = k0 + qk_width
    o0 = v0 + v_width
    gate0 = o0 + v_width
    cb0 = 0
    cc0 = cb0 + conv_width
    cx0 = cc0 + conv_width
    g0 = cx0 + conv_width

    xr = x.reshape(B * S, D)
    memr = mem.reshape(B * mem_len, D)
    for l in range(depth):
        w_tail = w_in[l, :, gate0 + 2 * H:].astype(BF16)
        w_gates = jnp.pad(w_in[l, :, gate0:gate0 + 2 * H],
                          ((0, 0), (0, GATE_LANES - 2 * H))).astype(BF16)

        h = prenorm(xr, norm_pre_mix[l])
        proj_a = matmul(h, w_in, BF16, 1024, 512, lead=(l,), n_cols=gate0)
        proj_b = matmul(h, w_tail, BF16, 1024, 1024)
        gates = matmul(h, w_gates, F32, 1024, GATE_LANES)
        gates_t = gates[:, :2 * H].T
        h_a = mlstm_mixer(proj_a, gates_t, b_if[l], mlstm_head_norm[l], B, S,
                          q0, k0, v0, o0, dqk, dv)
        h_b = gated_conv(proj_b, conv_w[l], B, S, cb0, cc0, cx0)
        merged = branch_merge(h_a, h_b, w_branch, (l,), proj_b, g0, 1024, 512)
        y = matmul(merged, w_mix_out, BF16, 1024, 512, lead=(l,))
        xr, h = post_norm_residual(y, xr, norm_post_mix[l], norm_pre_xattn[l])

        m = prenorm(memr, norm_mem[l])
        k = matmul(m, w_xk, BF16, B * mem_len, 512, lead=(l,))
        v = matmul(m, w_xv, BF16, B * mem_len, 512, lead=(l,))
        q = matmul(h, w_xq, BF16, 1024, 512, lead=(l,))
        o = cross_attention(q, k, v, B, S, mem_len, dh)
        y = matmul(o, w_xo, BF16, 1024, 512, lead=(l,))
        xr, h = post_norm_residual(y, xr, norm_post_xattn[l], norm_pre_ffn[l])

        hidden = swiglu_up(h, w_ffn_gate, w_ffn_up, 2048, 256, lead=(l,))
        y = matmul_ksplit(hidden, w_ffn_down[l].astype(BF16), BF16, 1024, 512, d_ff // 2)
        xr = post_norm_residual(y, xr, norm_post_ffn[l])
    return xr.reshape(B, S, D)
```

```python
import functools

import jax
import jax.numpy as jnp
from jax import lax
from jax.experimental import pallas as pl
from jax.experimental.pallas import tpu as pltpu

F32 = jnp.float32
BF16 = jnp.bfloat16

MLSTM_HEADS = 4
XATTN_HEADS = 4
CONV_K = 3
GATE_CAP = 15.0
EPS = 1e-6
MLSTM_CHUNK = 256
LANES = 128

VMEM_LIMIT_BYTES = 56 * 1024 * 1024


def _params(*sem):
    return pltpu.CompilerParams(dimension_semantics=sem,
                                vmem_limit_bytes=VMEM_LIMIT_BYTES)


def _rms(x, w):
    return x * lax.rsqrt(jnp.mean(x * x, axis=-1, keepdims=True) + EPS) * w


def _prenorm_kernel(x_ref, w_ref, h_ref):
    h_ref[...] = _rms(x_ref[...].astype(F32), w_ref[...]).astype(h_ref.dtype)


def prenorm(x, w, tr=256):
    M, D = x.shape
    return pl.pallas_call(
        _prenorm_kernel,
        grid=(M // tr,),
        in_specs=[pl.BlockSpec((tr, D), lambda i: (i, 0)),
                  pl.BlockSpec((1, D), lambda i: (0, 0))],
        out_specs=pl.BlockSpec((tr, D), lambda i: (i, 0)),
        out_shape=jax.ShapeDtypeStruct((M, D), BF16),
        compiler_params=_params("parallel"),
        name="prenorm",
    )(x, w.reshape(1, D))


def _post_pre_kernel(y_ref, x_ref, wpost_ref, wpre_ref, xo_ref, h_ref):
    xn = x_ref[...] + _rms(y_ref[...].astype(F32), wpost_ref[...])
    xo_ref[...] = xn
    h_ref[...] = _rms(xn, wpre_ref[...]).astype(h_ref.dtype)


def _post_kernel(y_ref, x_ref, wpost_ref, xo_ref):
    xo_ref[...] = x_ref[...] + _rms(y_ref[...].astype(F32), wpost_ref[...])


def post_norm_residual(y, x, w_post, w_pre_next=None, tr=256):
    M, D = x.shape
    row = pl.BlockSpec((tr, D), lambda i: (i, 0))
    vec = pl.BlockSpec((1, D), lambda i: (0, 0))
    if w_pre_next is None:
        return pl.pallas_call(
            _post_kernel, grid=(M // tr,),
            in_specs=[row, row, vec], out_specs=row,
            out_shape=jax.ShapeDtypeStruct((M, D), F32),
            compiler_params=_params("parallel"), name="post_norm",
        )(y, x, w_post.reshape(1, D))
    return pl.pallas_call(
        _post_pre_kernel, grid=(M // tr,),
        in_specs=[row, row, vec, vec], out_specs=[row, row],
        out_shape=[jax.ShapeDtypeStruct((M, D), F32),
                   jax.ShapeDtypeStruct((M, D), BF16)],
        compiler_params=_params("parallel"), name="post_pre_norm",
    )(y, x, w_post.reshape(1, D), w_pre_next.reshape(1, D))


def _weight_spec(w, lead, k_rows, tn, index_map):
    assert w.ndim == len(lead) + 2
    return pl.BlockSpec((None,) * len(lead) + (k_rows, tn),
                        lambda *g: tuple(lead) + index_map(*g))


def _mm_kernel(a_ref, w_ref, o_ref):
    o_ref[...] = jnp.dot(a_ref[...], w_ref[...].astype(BF16),
                         preferred_element_type=F32).astype(o_ref.dtype)


def matmul(a, w, out_dtype, tm, tn, lead=(), col0=0, n_cols=None):
    M, K = a.shape
    N = w.shape[-1] - col0 if n_cols is None else n_cols
    assert w.shape[-2] == K and M % tm == 0 and N % tn == 0 and col0 % tn == 0
    jb = col0 // tn
    return pl.pallas_call(
        _mm_kernel,
        grid=(N // tn, M // tm),
        in_specs=[pl.BlockSpec((tm, K), lambda j, i: (i, 0)),
                  _weight_spec(w, lead, K, tn, lambda j, i: (0, jb + j))],
        out_specs=pl.BlockSpec((tm, tn), lambda j, i: (i, j)),
        out_shape=jax.ShapeDtypeStruct((M, N), out_dtype),
        compiler_params=_params("parallel", "parallel"),
        name="matmul",
    )(a, w)


def _mm_shifted_kernel(a_ref, w_ref, wnext_ref, o_ref, wb_ref, *, shift, rows):
    @pl.when(pl.program_id(1) == 0)
    def _():
        K, tn = w_ref.shape
        width = tn + wnext_ref.shape[1]

        def body(c, carry):
            r = pl.ds(pl.multiple_of(c * rows, rows), rows)
            cat = jnp.concatenate([w_ref[r, :], wnext_ref[r, :]], axis=1)
            wb_ref[r, :] = pltpu.roll(cat, width - shift, axis=1)[:, :tn].astype(BF16)
            return carry

        lax.fori_loop(0, K // rows, body, 0)

    o_ref[...] = jnp.dot(a_ref[...], wb_ref[...],
                         preferred_element_type=F32).astype(o_ref.dtype)


def matmul_shifted(a, w, out_dtype, tm, tn, lead, col0, shift, n_cols):
    M, K = a.shape
    lane = LANES
    assert w.shape[-2] == K and M % tm == 0 and n_cols % tn == 0
    assert col0 % tn == 0 and tn % lane == 0 and 0 < shift < lane
    assert col0 + shift + n_cols <= w.shape[-1]
    jb = col0 // tn
    nb = (col0 + tn) // lane
    step = tn // lane
    return pl.pallas_call(
        functools.partial(_mm_shifted_kernel, shift=shift, rows=256),
        grid=(n_cols // tn, M // tm),
        in_specs=[pl.BlockSpec((tm, K), lambda j, i: (i, 0)),
                  _weight_spec(w, lead, K, tn, lambda j, i: (0, jb + j)),
                  _weight_spec(w, lead, K, lane, lambda j, i: (0, nb + step * j))],
        out_specs=pl.BlockSpec((tm, tn), lambda j, i: (i, j)),
        out_shape=jax.ShapeDtypeStruct((M, n_cols), out_dtype),
        scratch_shapes=[pltpu.VMEM((K, tn), BF16)],
        compiler_params=_params("parallel", "arbitrary"),
        name="matmul_shifted",
    )(a, w, w)


def _mm_ksplit_kernel(a_ref, w_ref, o_ref, acc_ref):
    d = jnp.dot(a_ref[...], w_ref[...], preferred_element_type=F32)
    k = pl.program_id(2)

    @pl.when(k == 0)
    def _():
        acc_ref[...] = d

    last = pl.num_programs(2) - 1

    @pl.when((k > 0) & (k < last))
    def _():
        acc_ref[...] += d

    @pl.when(k == last)
    def _():
        o_ref[...] = (acc_ref[...] + d).astype(o_ref.dtype)


def matmul_ksplit(a, w, out_dtype, tm, tn, tk):
    M, K = a.shape
    N = w.shape[1]
    assert M % tm == 0 and N % tn == 0 and K % tk == 0 and K // tk >= 2
    return pl.pallas_call(
        _mm_ksplit_kernel,
        grid=(N // tn, M // tm, K // tk),
        in_specs=[pl.BlockSpec((tm, tk), lambda j, i, k: (i, k)),
                  pl.BlockSpec((tk, tn), lambda j, i, k: (k, j))],
        out_specs=pl.BlockSpec((tm, tn), lambda j, i, k: (i, j)),
        out_shape=jax.ShapeDtypeStruct((M, N), out_dtype),
        scratch_shapes=[pltpu.VMEM((tm, tn), F32)],
        compiler_params=_params("parallel", "parallel", "arbitrary"),
        name="matmul_ksplit",
    )(a, w)


def _swiglu_kernel(a_ref, wg_ref, wu_ref, o_ref):
    a = a_ref[...]
    g = jnp.dot(a, wg_ref[...].astype(BF16), preferred_element_type=F32)
    u = jnp.dot(a, wu_ref[...].astype(BF16), preferred_element_type=F32)
    o_ref[...] = (g * jax.nn.sigmoid(g) * u).astype(o_ref.dtype)


def swiglu_up(a, wg, wu, tm, tn, lead=()):
    M, K = a.shape
    N = wg.shape[-1]
    assert M % tm == 0 and N % tn == 0
    wspec = _weight_spec(wg, lead, K, tn, lambda i, j: (0, j))
    return pl.pallas_call(
        _swiglu_kernel,
        grid=(M // tm, N // tn),
        in_specs=[pl.BlockSpec((tm, K), lambda i, j: (i, 0)), wspec, wspec],
        out_specs=pl.BlockSpec((tm, tn), lambda i, j: (i, j)),
        out_shape=jax.ShapeDtypeStruct((M, N), BF16),
        compiler_params=_params("parallel", "parallel"),
        name="swiglu_up",
    )(a, wg, wu)


def _branch_kernel(a0_ref, a1_ref, w0_ref, w1_ref, g0_ref, g1_ref, o_ref):
    y0 = jnp.dot(a0_ref[...], w0_ref[...].astype(BF16), preferred_element_type=F32)
    y1 = jnp.dot(a1_ref[...], w1_ref[...].astype(BF16), preferred_element_type=F32)
    o_ref[...] = (jax.nn.sigmoid(g0_ref[...].astype(F32)) * y0
                  + jax.nn.sigmoid(g1_ref[...].astype(F32)) * y1).astype(o_ref.dtype)


def branch_merge(a0, a1, w, lead, proj, g_col0, tm, tn):
    M, K = a0.shape
    N = w.shape[-1]
    assert M % tm == 0 and N % tn == 0 and g_col0 % tn == 0
    gb0 = g_col0 // tn
    gb1 = (g_col0 + N) // tn
    aspec = pl.BlockSpec((tm, K), lambda j, i: (i, 0))
    return pl.pallas_call(
        _branch_kernel,
        grid=(N // tn, M // tm),
        in_specs=[aspec, aspec,
                  _weight_spec(w, tuple(lead) + (0,), K, tn, lambda j, i: (0, j)),
                  _weight_spec(w, tuple(lead) + (1,), K, tn, lambda j, i: (0, j)),
                  pl.BlockSpec((tm, tn), lambda j, i: (i, gb0 + j)),
                  pl.BlockSpec((tm, tn), lambda j, i: (i, gb1 + j))],
        out_specs=pl.BlockSpec((tm, tn), lambda j, i: (i, j)),
        out_shape=jax.ShapeDtypeStruct((M, N), BF16),
        compiler_params=_params("parallel", "parallel"),
        name="branch_merge",
    )(a0, a1, w, w, proj, proj)


def _mlstm_kernel(bias_ref, q_ref, k_ref, v_ref, o_ref, gt_ref, hn_ref, out_ref,
                  c_ref, n_ref, m_ref):
    H = MLSTM_HEADS
    L = q_ref.shape[0]
    dqk = q_ref.shape[1] // H
    dv = v_ref.shape[1] // H

    @pl.when(pl.program_id(1) == 0)
    def _():
        c_ref[...] = jnp.zeros_like(c_ref)
        n_ref[...] = jnp.zeros_like(n_ref)
        m_ref[...] = jnp.zeros_like(m_ref)

    pre = GATE_CAP * jnp.tanh((gt_ref[...] + bias_ref[...]) / GATE_CAP)
    lf = jnp.minimum(pre, 0.0) - jnp.log1p(jnp.exp(-jnp.abs(pre)))
    lane = lax.broadcasted_iota(jnp.int32, lf.shape, 1)
    b_all = lf
    sh = 1
    while sh < L:
        b_all = b_all + jnp.where(lane >= sh, pltpu.roll(b_all, sh, axis=1), 0.0)
        sh *= 2

    for h in range(H):
        out, c_new, n_new, m_new = _mlstm_head(
            pre[h:h + 1, :], b_all[H + h:H + h + 1, :],
            q_ref[:, h * dqk:(h + 1) * dqk], k_ref[:, h * dqk:(h + 1) * dqk],
            v_ref[:, h * dv:(h + 1) * dv], o_ref[:, h * dv:(h + 1) * dv],
            hn_ref[:, h * dv:(h + 1) * dv], c_ref[h], n_ref[h], m_ref[h][:, 0:1])
        out_ref[:, h * dv:(h + 1) * dv] = out.astype(out_ref.dtype)
        c_ref[h] = c_new
        n_ref[h] = n_new
        m_ref[h] = jnp.broadcast_to(m_new, m_ref.shape[1:])


def _mlstm_head(li, b_row, q, k, v, o_pre, head_norm, c_old, n_old, m_prev):
    L, dqk = q.shape
    scale = dqk ** -0.5
    b_sq = jnp.broadcast_to(b_row, (L, L)).T
    row = lax.broadcasted_iota(jnp.int32, (L, L), 0)
    col = lax.broadcasted_iota(jnp.int32, (L, L), 1)
    dm = jnp.where(col <= row, b_sq - b_row + li, -jnp.inf)
    inter = b_sq[:, 0:1] + m_prev
    m_row = jnp.maximum(inter, jnp.max(dm, axis=-1, keepdims=True))
    w_intra = jnp.exp(dm - m_row)
    s_inter = jnp.exp(inter - m_row)

    qk = lax.dot_general(q, k, (((1,), (1,)), ((), ())),
                         preferred_element_type=F32) * scale * w_intra
    q_c = jnp.dot(q, c_old.astype(BF16), preferred_element_type=F32) * scale
    num = s_inter * q_c + jnp.dot(qk.astype(BF16), v, preferred_element_type=F32)
    q_n = jnp.sum(q.astype(F32) * n_old, axis=-1, keepdims=True) * scale
    den = s_inter * q_n + jnp.sum(qk, axis=-1, keepdims=True)
    hh = num / jnp.maximum(jnp.abs(den), jnp.exp(-m_row))

    b_last = b_row[:, L - 1:L]
    a_row = b_last - b_row + li
    m_new = jnp.maximum(b_last + m_prev, jnp.max(a_row, axis=-1, keepdims=True))
    s_state = jnp.exp(b_last + m_prev - m_new)
    e_col = jnp.broadcast_to(jnp.exp(a_row - m_new), (L, L)).T[:, 0:1]
    kw = k.astype(F32) * e_col
    c_new = s_state * c_old + lax.dot_general(
        kw.astype(BF16), v, (((0,), (0,)), ((), ())), preferred_element_type=F32)
    n_new = s_state * n_old + jnp.sum(kw, axis=0, keepdims=True)

    out = jax.nn.sigmoid(o_pre.astype(F32)) * _rms(hh, head_norm)
    return out, c_new, n_new, m_new


def mlstm_mixer(proj, gates_t, b_if, head_norm, batch, seq, q_col0, k_col0, v_col0, o_col0,
                dqk, dv):
    L = MLSTM_CHUNK
    T = seq // L
    H = MLSTM_HEADS
    qkw, vw = H * dqk, H * dv
    assert q_col0 % qkw == 0 and k_col0 % qkw == 0 and v_col0 % vw == 0 and o_col0 % vw == 0
    qb, kb, vb, ob = q_col0 // qkw, k_col0 // qkw, v_col0 // vw, o_col0 // vw
    row = lambda cb: (lambda b, t: (b * T + t, cb))
    return pl.pallas_call(
        _mlstm_kernel,
        grid=(batch, T),
        in_specs=[
            pl.BlockSpec((2 * H, 1), lambda b, t: (0, 0)),
            pl.BlockSpec((L, qkw), row(qb)),
            pl.BlockSpec((L, qkw), row(kb)),
            pl.BlockSpec((L, vw), row(vb)),
            pl.BlockSpec((L, vw), row(ob)),
            pl.BlockSpec((2 * H, L), lambda b, t: (0, b * T + t)),
            pl.BlockSpec((1, vw), lambda b, t: (0, 0)),
        ],
        out_specs=pl.BlockSpec((L, vw), row(0)),
        out_shape=jax.ShapeDtypeStruct((batch * seq, vw), BF16),
        scratch_shapes=[pltpu.VMEM((H, dqk, dv), F32),
                        pltpu.VMEM((H, 1, dqk), F32),
                        pltpu.VMEM((H, 1, LANES), F32)],
        compiler_params=_params("parallel", "arbitrary"),
        name="mlstm",
    )(b_if.reshape(2 * H, 1), proj, proj, proj, proj, gates_t, head_norm.reshape(1, vw))


def _conv_kernel(cb_ref, cc_ref, cx_ref, w_ref, o_ref, carry_ref):
    s = pl.program_id(2)
    ts = cb_ref.shape[0]

    @pl.when(s == 0)
    def _():
        carry_ref[...] = jnp.zeros_like(carry_ref)

    w = w_ref[...]
    w0, w1, w2 = w[0:1, :], w[1:2, :], w[2:3, :]
    u = cc_ref[...].astype(F32) * cx_ref[...].astype(F32)
    y = w2 * u + w1 * pltpu.roll(u, 1, axis=0) + w0 * pltpu.roll(u, 2, axis=0)
    o_ref[...] = (cb_ref[...].astype(F32) * y).astype(o_ref.dtype)

    prev = carry_ref[...]
    u_h = u[0:8, :]
    r = lax.broadcasted_iota(jnp.int32, u_h.shape, 0)
    u1 = jnp.where(r < 1, pltpu.roll(prev, 1, axis=0), pltpu.roll(u_h, 1, axis=0))
    u2 = jnp.where(r < 2, pltpu.roll(prev, 2, axis=0), pltpu.roll(u_h, 2, axis=0))
    y_h = w2 * u_h + w1 * u1 + w0 * u2
    o_ref[0:8, :] = (cb_ref[0:8, :].astype(F32) * y_h).astype(o_ref.dtype)
    carry_ref[...] = u[ts - 8:ts, :]


def gated_conv(proj, conv_w, batch, seq, cb_col0, cc_col0, cx_col0, ts=512, tc=512):
    C = conv_w.shape[1]
    nS = seq // ts
    b0, c0, x0 = cb_col0 // tc, cc_col0 // tc, cx_col0 // tc
    return pl.pallas_call(
        _conv_kernel,
        grid=(batch, C // tc, nS),
        in_specs=[
            pl.BlockSpec((ts, tc), lambda b, c, s: (b * nS + s, b0 + c)),
            pl.BlockSpec((ts, tc), lambda b, c, s: (b * nS + s, c0 + c)),
            pl.BlockSpec((ts, tc), lambda b, c, s: (b * nS + s, x0 + c)),
            pl.BlockSpec((CONV_K, tc), lambda b, c, s: (0, c)),
        ],
        out_specs=pl.BlockSpec((ts, tc), lambda b, c, s: (b * nS + s, c)),
        out_shape=jax.ShapeDtypeStruct((batch * seq, C), BF16),
        scratch_shapes=[pltpu.VMEM((8, tc), F32)],
        compiler_params=_params("parallel", "parallel", "arbitrary"),
        name="gated_conv",
    )(proj, proj, proj, conv_w)


def _xattn_kernel(q_ref, k_ref, v_ref, o_ref):
    dh = q_ref.shape[1]
    s = lax.dot_general(q_ref[...], k_ref[...], (((1,), (1,)), ((), ())),
                        preferred_element_type=F32) * (dh ** -0.5)
    e = jnp.exp(s - jnp.max(s, axis=-1, keepdims=True))
    p = e / jnp.sum(e, axis=-1, keepdims=True)
    o_ref[...] = jnp.dot(p.astype(BF16), v_ref[...],
                         preferred_element_type=F32).astype(o_ref.dtype)


def cross_attention(q, k, v, batch, seq, mem_len, dh, ts=512):
    H = XATTN_HEADS
    nS = seq // ts
    return pl.pallas_call(
        _xattn_kernel,
        grid=(batch, H, nS),
        in_specs=[
            pl.BlockSpec((ts, dh), lambda b, h, s: (b * nS + s, h)),
            pl.BlockSpec((mem_len, dh), lambda b, h, s: (b, h)),
            pl.BlockSpec((mem_len, dh), lambda b, h, s: (b, h)),
        ],
        out_specs=pl.BlockSpec((ts, dh), lambda b, h, s: (b * nS + s, h)),
        out_shape=jax.ShapeDtypeStruct((batch * seq, H * dh), BF16),
        compiler_params=_params("parallel", "parallel", "parallel"),
        name="cross_attention",
    )(q, k, v)


def kernel(x, mem, norm_pre_mix, norm_post_mix, w_in, b_if, mlstm_head_norm, conv_w, w_branch, w_mix_out, norm_pre_xattn, norm_post_xattn, norm_mem, w_xq, w_xk, w_xv, w_xo, norm_pre_ffn, norm_post_ffn, w_ffn_gate, w_ffn_up, w_ffn_down):
    B, S, D = x.shape
    mem_len = mem.shape[1]
    depth = w_in.shape[0]
    H = MLSTM_HEADS
    v_width = mlstm_head_norm.shape[1]
    conv_width = conv_w.shape[2]
    dv = v_width // H
    dqk = dv // 2
    qk_width = H * dqk
    dh = w_xq.shape[2] // XATTN_HEADS
    d_ff = w_ffn_gate.shape[2]

    q0 = 0
    k0 = q0 + qk_width
    v0 = k0 + qk_width
    o0 = v0 + v_width
    gate0 = o0 + v_width
    cb0 = 0
    cc0 = cb0 + conv_width
    cx0 = cc0 + conv_width
    g0 = cx0 + conv_width

    xr = x.reshape(B * S, D)
    memr = mem.reshape(B * mem_len, D)
    for l in range(depth):
        h = prenorm(xr, norm_pre_mix[l])
        proj_a = matmul(h, w_in, BF16, 1024, 512, lead=(l,), n_cols=gate0)
        proj_b = matmul_shifted(h, w_in, BF16, 1024, 512, (l,), gate0, 2 * H,
                                w_in.shape[-1] - gate0 - 2 * H)
        gates = matmul(h, w_in, F32, 1024, LANES, lead=(l,), col0=gate0, n_cols=LANES)
        gates_t = gates[:, :2 * H].T
        h_a = mlstm_mixer(proj_a, gates_t, b_if[l], mlstm_head_norm[l], B, S,
                          q0, k0, v0, o0, dqk, dv)
        h_b = gated_conv(proj_b, conv_w[l], B, S, cb0, cc0, cx0)
        merged = branch_merge(h_a, h_b, w_branch, (l,), proj_b, g0, 1024, 512)
        y = matmul(merged, w_mix_out, BF16, 1024, 512, lead=(l,))
        xr, h = post_norm_residual(y, xr, norm_post_mix[l], norm_pre_xattn[l])

        m = prenorm(memr, norm_mem[l])
        k = matmul(m, w_xk, BF16, B * mem_len, 512, lead=(l,))
        v = matmul(m, w_xv, BF16, B * mem_len, 512, lead=(l,))
        q = matmul(h, w_xq, BF16, 1024, 512, lead=(l,))
        o = cross_attention(q, k, v, B, S, mem_len, dh)
        y = matmul(o, w_xo, BF16, 1024, 512, lead=(l,))
        xr, h = post_norm_residual(y, xr, norm_post_xattn[l], norm_pre_ffn[l])

        hidden = swiglu_up(h, w_ffn_gate, w_ffn_up, 2048, 256, lead=(l,))
        y = matmul_ksplit(hidden, w_ffn_down[l].astype(BF16), BF16, 1024, 512, d_ff // 2)
        xr = post_norm_residual(y, xr, norm_post_ffn[l])
    return xr.reshape(B, S, D)
```

```python
import functools

import jax
import jax.numpy as jnp
from jax import lax
from jax.experimental import pallas as pl
from jax.experimental.pallas import tpu as pltpu

F32 = jnp.float32
BF16 = jnp.bfloat16

MLSTM_HEADS = 4
XATTN_HEADS = 4
CONV_K = 3
GATE_CAP = 15.0
EPS = 1e-6
MLSTM_CHUNK = 256
LANES = 128
SUBLANES = 8
NORM_ROWS = 32

VMEM_LIMIT_BYTES = 56 * 1024 * 1024


def _params(*sem):
    return pltpu.CompilerParams(dimension_semantics=sem,
                                vmem_limit_bytes=VMEM_LIMIT_BYTES)


def _rms(x, w):
    return x * lax.rsqrt(jnp.mean(x * x, axis=-1, keepdims=True) + EPS) * w


def _prenorm_kernel(x_ref, w_ref, h_ref):
    h_ref[...] = _rms(x_ref[...].astype(F32), w_ref[...]).astype(h_ref.dtype)


def prenorm(x, w, tr=256):
    M, D = x.shape
    return pl.pallas_call(
        _prenorm_kernel,
        grid=(M // tr,),
        in_specs=[pl.BlockSpec((tr, D), lambda i: (i, 0)),
                  pl.BlockSpec((1, D), lambda i: (0, 0))],
        out_specs=pl.BlockSpec((tr, D), lambda i: (i, 0)),
        out_shape=jax.ShapeDtypeStruct((M, D), BF16),
        compiler_params=_params("parallel"),
        name="prenorm",
    )(x, w.reshape(1, D))


def _post_pre_kernel(y_ref, x_ref, wpost_ref, wpre_ref, xo_ref, h_ref):
    xn = x_ref[...] + _rms(y_ref[...].astype(F32), wpost_ref[...])
    xo_ref[...] = xn
    h_ref[...] = _rms(xn, wpre_ref[...]).astype(h_ref.dtype)


def _post_kernel(y_ref, x_ref, wpost_ref, xo_ref):
    xo_ref[...] = x_ref[...] + _rms(y_ref[...].astype(F32), wpost_ref[...])


def post_norm_residual(y, x, w_post, w_pre_next=None, tr=256):
    M, D = x.shape
    row = pl.BlockSpec((tr, D), lambda i: (i, 0))
    vec = pl.BlockSpec((1, D), lambda i: (0, 0))
    if w_pre_next is None:
        return pl.pallas_call(
            _post_kernel, grid=(M // tr,),
            in_specs=[row, row, vec], out_specs=row,
            out_shape=jax.ShapeDtypeStruct((M, D), F32),
            compiler_params=_params("parallel"), name="post_norm",
        )(y, x, w_post.reshape(1, D))
    return pl.pallas_call(
        _post_pre_kernel, grid=(M // tr,),
        in_specs=[row, row, vec, vec], out_specs=[row, row],
        out_shape=[jax.ShapeDtypeStruct((M, D), F32),
                   jax.ShapeDtypeStruct((M, D), BF16)],
        compiler_params=_params("parallel"), name="post_pre_norm",
    )(y, x, w_post.reshape(1, D), w_pre_next.reshape(1, D))


def _weight_spec(w, lead, k_rows, tn, index_map):
    assert w.ndim == len(lead) + 2
    return pl.BlockSpec((None,) * len(lead) + (k_rows, tn),
                        lambda *g: tuple(lead) + index_map(*g))


def _mm_kernel(a_ref, w_ref, o_ref):
    o_ref[...] = jnp.dot(a_ref[...], w_ref[...].astype(BF16),
                         preferred_element_type=F32).astype(o_ref.dtype)


def matmul(a, w, out_dtype, tm, tn, lead=(), m_outer=False):
    M, K = a.shape
    N = w.shape[-1]
    assert w.shape[-2] == K and M % tm == 0 and N % tn == 0
    if m_outer:
        grid, ij = (M // tm, N // tn), (lambda i, j: (i, j))
    else:
        grid, ij = (N // tn, M // tm), (lambda j, i: (i, j))
    return pl.pallas_call(
        _mm_kernel,
        grid=grid,
        in_specs=[pl.BlockSpec((tm, K), lambda *g: (ij(*g)[0], 0)),
                  _weight_spec(w, lead, K, tn, lambda *g: (0, ij(*g)[1]))],
        out_specs=pl.BlockSpec((tm, tn), lambda *g: ij(*g)),
        out_shape=jax.ShapeDtypeStruct((M, N), out_dtype),
        compiler_params=_params("parallel", "parallel"),
        name="matmul",
    )(a, w)


def _mm_wt_kernel(a_ref, wt_ref, o_ref):
    o_ref[...] = lax.dot_general(a_ref[...], wt_ref[...].astype(BF16),
                                 (((1,), (1,)), ((), ())),
                                 preferred_element_type=F32).astype(o_ref.dtype)


def matmul_wt(a, wt, out_dtype, tm, tn, row0, n_rows):
    M, K = a.shape
    assert wt.shape[1] == K and M % tm == 0 and n_rows % tn == 0
    assert row0 % SUBLANES == 0 and row0 + n_rows <= wt.shape[0]
    return pl.pallas_call(
        _mm_wt_kernel,
        grid=(n_rows // tn, M // tm),
        in_specs=[pl.BlockSpec((tm, K), lambda j, i: (i, 0)),
                  pl.BlockSpec((pl.Element(tn), pl.Element(K)),
                               lambda j, i: (pl.multiple_of(row0 + j * tn, SUBLANES), 0))],
        out_specs=pl.BlockSpec((tm, tn), lambda j, i: (i, j)),
        out_shape=jax.ShapeDtypeStruct((M, n_rows), out_dtype),
        compiler_params=_params("parallel", "parallel"),
        name="matmul_wt",
    )(a, wt)


def _mm_postnorm_kernel(a_ref, w_ref, x_ref, wpost_ref, wpre_ref, xo_ref, h_ref, y_scr,
                        *, n_m):
    i = pl.program_id(0)
    j = pl.program_id(1)
    n_n, _, tn = y_scr.shape[1:]
    rc, d = x_ref.shape

    def mm():
        y_scr[i % 2, j] = jnp.dot(a_ref[...], w_ref[...].astype(BF16),
                                  preferred_element_type=F32).astype(y_scr.dtype)

    def norm():
        slot = (i + 1) % 2
        cols = [slice(n * tn, (n + 1) * tn) for n in range(n_n)]
        for s in range(rc // NORM_ROWS):
            rs = slice(s * NORM_ROWS, (s + 1) * NORM_ROWS)
            rows = pl.ds(pl.multiple_of(j * rc, rc) + s * NORM_ROWS, NORM_ROWS)
            y_tile = lambda n: y_scr[slot, n, rows, :].astype(F32)
            ssq = None
            for n in range(n_n):
                y = y_tile(n)
                ssq = y * y if ssq is None else ssq + y * y
            r_post = lax.rsqrt(jnp.sum(ssq, axis=-1, keepdims=True) / d + EPS)
            ssq = None
            for n, c in enumerate(cols):
                xn = x_ref[rs, c] + y_tile(n) * r_post * wpost_ref[:, c]
                xo_ref[rs, c] = xn
                ssq = xn * xn if ssq is None else ssq + xn * xn
            r_pre = lax.rsqrt(jnp.sum(ssq, axis=-1, keepdims=True) / d + EPS)
            for c in cols:
                h_ref[rs, c] = (xo_ref[rs, c] * r_pre * wpre_ref[:, c]).astype(h_ref.dtype)

    @pl.when(i == 0)
    def _():
        mm()

    @pl.when((i > 0) & (i < n_m))
    def _():
        norm()
        mm()

    @pl.when(i == n_m)
    def _():
        norm()


def matmul_post_pre_norm(a, w, x, w_post, w_pre, tm, tn):
    M, K = a.shape
    D = w.shape[1]
    n_m, n_n = M // tm, D // tn
    rc = tm // n_n
    assert M % tm == 0 and D % tn == 0 and tm % n_n == 0 and rc % (2 * SUBLANES) == 0
    chunk = pl.BlockSpec(
        (rc, D), lambda i, j: (jnp.maximum(i - 1, 0) * n_n + jnp.where(i == 0, 0, j), 0))
    vec = pl.BlockSpec((1, D), lambda i, j: (0, 0))
    return pl.pallas_call(
        functools.partial(_mm_postnorm_kernel, n_m=n_m),
        grid=(n_m + 1, n_n),
        in_specs=[pl.BlockSpec((tm, K), lambda i, j: (jnp.minimum(i, n_m - 1), 0)),
                  pl.BlockSpec((K, tn), lambda i, j: (0, jnp.where(i == n_m, n_n - 1, j))),
                  chunk, vec, vec],
        out_specs=[chunk, chunk],
        out_shape=[jax.ShapeDtypeStruct((M, D), F32), jax.ShapeDtypeStruct((M, D), BF16)],
        scratch_shapes=[pltpu.VMEM((2, n_n, tm, tn), BF16)],
        compiler_params=_params("arbitrary", "arbitrary"),
        name="matmul_post_pre_norm",
    )(a, w, x, w_post.reshape(1, D), w_pre.reshape(1, D))


def _swiglu_kernel(a_ref, wg_ref, wu_ref, o_ref):
    a = a_ref[...]
    g = jnp.dot(a, wg_ref[...].astype(BF16), preferred_element_type=F32)
    u = jnp.dot(a, wu_ref[...].astype(BF16), preferred_element_type=F32)
    o_ref[...] = (g * jax.nn.sigmoid(g) * u).astype(o_ref.dtype)


def swiglu_up(a, wg, wu, tm, tn, lead=()):
    M, K = a.shape
    N = wg.shape[-1]
    assert M % tm == 0 and N % tn == 0
    wspec = _weight_spec(wg, lead, K, tn, lambda i, j: (0, j))
    return pl.pallas_call(
        _swiglu_kernel,
        grid=(M // tm, N // tn),
        in_specs=[pl.BlockSpec((tm, K), lambda i, j: (i, 0)), wspec, wspec],
        out_specs=pl.BlockSpec((tm, tn), lambda i, j: (i, j)),
        out_shape=jax.ShapeDtypeStruct((M, N), BF16),
        compiler_params=_params("parallel", "parallel"),
        name="swiglu_up",
    )(a, wg, wu)


def _branch_kernel(a0_ref, a1_ref, w0_ref, w1_ref, g0_ref, g1_ref, o_ref):
    y0 = jnp.dot(a0_ref[...], w0_ref[...].astype(BF16), preferred_element_type=F32)
    y1 = jnp.dot(a1_ref[...], w1_ref[...].astype(BF16), preferred_element_type=F32)
    o_ref[...] = (jax.nn.sigmoid(g0_ref[...].astype(F32)) * y0
                  + jax.nn.sigmoid(g1_ref[...].astype(F32)) * y1).astype(o_ref.dtype)


def branch_merge(a0, a1, w, lead, proj, g_col0, tm, tn):
    M, K = a0.shape
    N = w.shape[-1]
    assert M % tm == 0 and N % tn == 0 and g_col0 % tn == 0
    gb0 = g_col0 // tn
    gb1 = (g_col0 + N) // tn
    aspec = pl.BlockSpec((tm, K), lambda j, i: (i, 0))
    return pl.pallas_call(
        _branch_kernel,
        grid=(N // tn, M // tm),
        in_specs=[aspec, aspec,
                  _weight_spec(w, tuple(lead) + (0,), K, tn, lambda j, i: (0, j)),
                  _weight_spec(w, tuple(lead) + (1,), K, tn, lambda j, i: (0, j)),
                  pl.BlockSpec((tm, tn), lambda j, i: (i, gb0 + j)),
                  pl.BlockSpec((tm, tn), lambda j, i: (i, gb1 + j))],
        out_specs=pl.BlockSpec((tm, tn), lambda j, i: (i, j)),
        out_shape=jax.ShapeDtypeStruct((M, N), BF16),
        compiler_params=_params("parallel", "parallel"),
        name="branch_merge",
    )(a0, a1, w, w, proj, proj)


def _mlstm_kernel(bias_ref, q_ref, k_ref, v_ref, o_ref, gt_ref, hn_ref, out_ref,
                  c_ref, n_ref, m_ref):
    H = MLSTM_HEADS
    L = q_ref.shape[0]
    dqk = q_ref.shape[1] // H
    dv = v_ref.shape[1] // H

    @pl.when(pl.program_id(1) == 0)
    def _():
        c_ref[...] = jnp.zeros_like(c_ref)
        n_ref[...] = jnp.zeros_like(n_ref)
        m_ref[...] = jnp.zeros_like(m_ref)

    pre = GATE_CAP * jnp.tanh((gt_ref[...] + bias_ref[...]) / GATE_CAP)
    lf = jnp.minimum(pre, 0.0) - jnp.log1p(jnp.exp(-jnp.abs(pre)))
    lane = lax.broadcasted_iota(jnp.int32, lf.shape, 1)
    b_all = lf
    sh = 1
    while sh < L:
        b_all = b_all + jnp.where(lane >= sh, pltpu.roll(b_all, sh, axis=1), 0.0)
        sh *= 2

    for h in range(H):
        out, c_new, n_new, m_new = _mlstm_head(
            pre[h:h + 1, :], b_all[H + h:H + h + 1, :],
            q_ref[:, h * dqk:(h + 1) * dqk], k_ref[:, h * dqk:(h + 1) * dqk],
            v_ref[:, h * dv:(h + 1) * dv], o_ref[:, h * dv:(h + 1) * dv],
            hn_ref[:, h * dv:(h + 1) * dv], c_ref[h], n_ref[h], m_ref[h][:, 0:1])
        out_ref[:, h * dv:(h + 1) * dv] = out.astype(out_ref.dtype)
        c_ref[h] = c_new
        n_ref[h] = n_new
        m_ref[h] = jnp.broadcast_to(m_new, m_ref.shape[1:])


def _mlstm_head(li, b_row, q, k, v, o_pre, head_norm, c_old, n_old, m_prev):
    L, dqk = q.shape
    scale = dqk ** -0.5
    b_sq = jnp.broadcast_to(b_row, (L, L)).T
    row = lax.broadcasted_iota(jnp.int32, (L, L), 0)
    col = lax.broadcasted_iota(jnp.int32, (L, L), 1)
    dm = jnp.where(col <= row, b_sq - b_row + li, -jnp.inf)
    inter = b_sq[:, 0:1] + m_prev
    m_row = jnp.maximum(inter, jnp.max(dm, axis=-1, keepdims=True))
    w_intra = jnp.exp(dm - m_row)
    s_inter = jnp.exp(inter - m_row)

    qk = lax.dot_general(q, k, (((1,), (1,)), ((), ())),
                         preferred_element_type=F32) * scale * w_intra
    q_c = jnp.dot(q, c_old.astype(BF16), preferred_element_type=F32) * scale
    num = s_inter * q_c + jnp.dot(qk.astype(BF16), v, preferred_element_type=F32)
    q_n = jnp.sum(q.astype(F32) * n_old, axis=-1, keepdims=True) * scale
    den = s_inter * q_n + jnp.sum(qk, axis=-1, keepdims=True)
    hh = num / jnp.maximum(jnp.abs(den), jnp.exp(-m_row))

    b_last = b_row[:, L - 1:L]
    a_row = b_last - b_row + li
    m_new = jnp.maximum(b_last + m_prev, jnp.max(a_row, axis=-1, keepdims=True))
    s_state = jnp.exp(b_last + m_prev - m_new)
    e_col = jnp.broadcast_to(jnp.exp(a_row - m_new), (L, L)).T[:, 0:1]
    kw = k.astype(F32) * e_col
    c_new = s_state * c_old + lax.dot_general(
        kw.astype(BF16), v, (((0,), (0,)), ((), ())), preferred_element_type=F32)
    n_new = s_state * n_old + jnp.sum(kw, axis=0, keepdims=True)

    out = jax.nn.sigmoid(o_pre.astype(F32)) * _rms(hh, head_norm)
    return out, c_new, n_new, m_new


def mlstm_mixer(proj, gates_t, b_if, head_norm, batch, seq, q_col0, k_col0, v_col0, o_col0,
                dqk, dv):
    L = MLSTM_CHUNK
    T = seq // L
    H = MLSTM_HEADS
    qkw, vw = H * dqk, H * dv
    assert q_col0 % qkw == 0 and k_col0 % qkw == 0 and v_col0 % vw == 0 and o_col0 % vw == 0
    qb, kb, vb, ob = q_col0 // qkw, k_col0 // qkw, v_col0 // vw, o_col0 // vw
    row = lambda cb: (lambda b, t: (b * T + t, cb))
    return pl.pallas_call(
        _mlstm_kernel,
        grid=(batch, T),
        in_specs=[
            pl.BlockSpec((2 * H, 1), lambda b, t: (0, 0)),
            pl.BlockSpec((L, qkw), row(qb)),
            pl.BlockSpec((L, qkw), row(kb)),
            pl.BlockSpec((L, vw), row(vb)),
            pl.BlockSpec((L, vw), row(ob)),
            pl.BlockSpec((2 * H, L), lambda b, t: (0, b * T + t)),
            pl.BlockSpec((1, vw), lambda b, t: (0, 0)),
        ],
        out_specs=pl.BlockSpec((L, vw), row(0)),
        out_shape=jax.ShapeDtypeStruct((batch * seq, vw), BF16),
        scratch_shapes=[pltpu.VMEM((H, dqk, dv), F32),
                        pltpu.VMEM((H, 1, dqk), F32),
                        pltpu.VMEM((H, 1, LANES), F32)],
        compiler_params=_params("parallel", "arbitrary"),
        name="mlstm",
    )(b_if.reshape(2 * H, 1), proj, proj, proj, proj, gates_t, head_norm.reshape(1, vw))


def _conv_kernel(cb_ref, cc_ref, cx_ref, w_ref, o_ref, carry_ref):
    s = pl.program_id(2)
    ts = cb_ref.shape[0]

    @pl.when(s == 0)
    def _():
        carry_ref[...] = jnp.zeros_like(carry_ref)

    w = w_ref[...]
    w0, w1, w2 = w[0:1, :], w[1:2, :], w[2:3, :]
    u = cc_ref[...].astype(F32) * cx_ref[...].astype(F32)
    y = w2 * u + w1 * pltpu.roll(u, 1, axis=0) + w0 * pltpu.roll(u, 2, axis=0)
    o_ref[...] = (cb_ref[...].astype(F32) * y).astype(o_ref.dtype)

    prev = carry_ref[...]
    u_h = u[0:8, :]
    r = lax.broadcasted_iota(jnp.int32, u_h.shape, 0)
    u1 = jnp.where(r < 1, pltpu.roll(prev, 1, axis=0), pltpu.roll(u_h, 1, axis=0))
    u2 = jnp.where(r < 2, pltpu.roll(prev, 2, axis=0), pltpu.roll(u_h, 2, axis=0))
    y_h = w2 * u_h + w1 * u1 + w0 * u2
    o_ref[0:8, :] = (cb_ref[0:8, :].astype(F32) * y_h).astype(o_ref.dtype)
    carry_ref[...] = u[ts - 8:ts, :]


def gated_conv(proj, conv_w, batch, seq, cb_col0, cc_col0, cx_col0, ts=512, tc=512):
    C = conv_w.shape[1]
    nS = seq // ts
    b0, c0, x0 = cb_col0 // tc, cc_col0 // tc, cx_col0 // tc
    return pl.pallas_call(
        _conv_kernel,
        grid=(batch, C // tc, nS),
        in_specs=[
            pl.BlockSpec((ts, tc), lambda b, c, s: (b * nS + s, b0 + c)),
            pl.BlockSpec((ts, tc), lambda b, c, s: (b * nS + s, c0 + c)),
            pl.BlockSpec((ts, tc), lambda b, c, s: (b * nS + s, x0 + c)),
            pl.BlockSpec((CONV_K, tc), lambda b, c, s: (0, c)),
        ],
        out_specs=pl.BlockSpec((ts, tc), lambda b, c, s: (b * nS + s, c)),
        out_shape=jax.ShapeDtypeStruct((batch * seq, C), BF16),
        scratch_shapes=[pltpu.VMEM((8, tc), F32)],
        compiler_params=_params("parallel", "parallel", "arbitrary"),
        name="gated_conv",
    )(proj, proj, proj, conv_w)


def _xattn_kernel(q_ref, k_ref, v_ref, o_ref):
    dh = q_ref.shape[1]
    s = lax.dot_general(q_ref[...], k_ref[...], (((1,), (1,)), ((), ())),
                        preferred_element_type=F32) * (dh ** -0.5)
    e = jnp.exp(s - jnp.max(s, axis=-1, keepdims=True))
    p = e / jnp.sum(e, axis=-1, keepdims=True)
    o_ref[...] = jnp.dot(p.astype(BF16), v_ref[...],
                         preferred_element_type=F32).astype(o_ref.dtype)


def cross_attention(q, k, v, batch, seq, mem_len, dh, ts=512):
    H = XATTN_HEADS
    nS = seq // ts
    return pl.pallas_call(
        _xattn_kernel,
        grid=(batch, H, nS),
        in_specs=[
            pl.BlockSpec((ts, dh), lambda b, h, s: (b * nS + s, h)),
            pl.BlockSpec((mem_len, dh), lambda b, h, s: (b, h)),
            pl.BlockSpec((mem_len, dh), lambda b, h, s: (b, h)),
        ],
        out_specs=pl.BlockSpec((ts, dh), lambda b, h, s: (b * nS + s, h)),
        out_shape=jax.ShapeDtypeStruct((batch * seq, H * dh), BF16),
        compiler_params=_params("parallel", "parallel", "parallel"),
        name="cross_attention",
    )(q, k, v)


def kernel(x, mem, norm_pre_mix, norm_post_mix, w_in, b_if, mlstm_head_norm, conv_w, w_branch, w_mix_out, norm_pre_xattn, norm_post_xattn, norm_mem, w_xq, w_xk, w_xv, w_xo, norm_pre_ffn, norm_post_ffn, w_ffn_gate, w_ffn_up, w_ffn_down):
    B, S, D = x.shape
    mem_len = mem.shape[1]
    depth = w_in.shape[0]
    H = MLSTM_HEADS
    v_width = mlstm_head_norm.shape[1]
    conv_width = conv_w.shape[2]
    dv = v_width // H
    dqk = dv // 2
    qk_width = H * dqk
    dh = w_xq.shape[2] // XATTN_HEADS
    d_ff = w_ffn_gate.shape[2]

    in_width = w_in.shape[2]
    w_in_t = jnp.swapaxes(w_in, 1, 2).reshape(depth * in_width, D)
    q0 = 0
    k0 = q0 + qk_width
    v0 = k0 + qk_width
    o0 = v0 + v_width
    gate0 = o0 + v_width
    cb0 = 0
    cc0 = cb0 + conv_width
    cx0 = cc0 + conv_width
    g0 = cx0 + conv_width

    xr = x.reshape(B * S, D)
    memr = mem.reshape(B * mem_len, D)
    for l in range(depth):
        h = prenorm(xr, norm_pre_mix[l])
        proj_a = matmul_wt(h, w_in_t, BF16, 1024, 512, l * in_width, gate0)
        proj_b = matmul_wt(h, w_in_t, BF16, 1024, 512, l * in_width + gate0 + 2 * H,
                           in_width - gate0 - 2 * H)
        gates_t = matmul_wt(h, w_in_t, F32, 1024, 2 * H, l * in_width + gate0, 2 * H).T
        h_a = mlstm_mixer(proj_a, gates_t, b_if[l], mlstm_head_norm[l], B, S,
                          q0, k0, v0, o0, dqk, dv)
        h_b = gated_conv(proj_b, conv_w[l], B, S, cb0, cc0, cx0)
        merged = branch_merge(h_a, h_b, w_branch, (l,), proj_b, g0, 1024, 512)
        xr, h = matmul_post_pre_norm(merged, w_mix_out[l].astype(BF16), xr,
                                     norm_post_mix[l], norm_pre_xattn[l], 1024, 512)

        m = prenorm(memr, norm_mem[l])
        k = matmul(m, w_xk, BF16, B * mem_len, 512, lead=(l,))
        v = matmul(m, w_xv, BF16, B * mem_len, 512, lead=(l,))
        q = matmul(h, w_xq, BF16, 1024, 512, lead=(l,))
        o = cross_attention(q, k, v, B, S, mem_len, dh)
        xr, h = matmul_post_pre_norm(o, w_xo[l].astype(BF16), xr,
                                     norm_post_xattn[l], norm_pre_ffn[l], 1024, 512)

        hidden = swiglu_up(h, w_ffn_gate, w_ffn_up, 2048, 256, lead=(l,))
        y = matmul(hidden, w_ffn_down[l].astype(BF16), BF16, 512, 512, m_outer=True)
        xr = post_norm_residual(y, xr, norm_post_ffn[l])
    return xr.reshape(B, S, D)
```

```python
import functools

import jax
import jax.numpy as jnp
from jax import lax
from jax.experimental import pallas as pl
from jax.experimental.pallas import tpu as pltpu

F32 = jnp.float32
BF16 = jnp.bfloat16

MLSTM_HEADS = 4
XATTN_HEADS = 4
CONV_K = 3
GATE_CAP = 15.0
EPS = 1e-6
MLSTM_CHUNK = 256
LANES = 128
SUBLANES = 8
NORM_ROWS = 32

VMEM_LIMIT_BYTES = 56 * 1024 * 1024


def _params(*sem):
    return pltpu.CompilerParams(dimension_semantics=sem,
                                vmem_limit_bytes=VMEM_LIMIT_BYTES)


def _rms(x, w):
    return x * lax.rsqrt(jnp.mean(x * x, axis=-1, keepdims=True) + EPS) * w


def _prenorm_kernel(x_ref, w_ref, h_ref):
    h_ref[...] = _rms(x_ref[...].astype(F32), w_ref[...]).astype(h_ref.dtype)


def prenorm(x, w, tr=256):
    M, D = x.shape
    return pl.pallas_call(
        _prenorm_kernel,
        grid=(M // tr,),
        in_specs=[pl.BlockSpec((tr, D), lambda i: (i, 0)),
                  pl.BlockSpec((1, D), lambda i: (0, 0))],
        out_specs=pl.BlockSpec((tr, D), lambda i: (i, 0)),
        out_shape=jax.ShapeDtypeStruct((M, D), BF16),
        compiler_params=_params("parallel"),
        name="prenorm",
    )(x, w.reshape(1, D))


def _post_pre_kernel(y_ref, x_ref, wpost_ref, wpre_ref, xo_ref, h_ref):
    xn = x_ref[...] + _rms(y_ref[...].astype(F32), wpost_ref[...])
    xo_ref[...] = xn
    h_ref[...] = _rms(xn, wpre_ref[...]).astype(h_ref.dtype)


def _post_kernel(y_ref, x_ref, wpost_ref, xo_ref):
    xo_ref[...] = x_ref[...] + _rms(y_ref[...].astype(F32), wpost_ref[...])


def post_norm_residual(y, x, w_post, w_pre_next=None, tr=256):
    M, D = x.shape
    row = pl.BlockSpec((tr, D), lambda i: (i, 0))
    vec = pl.BlockSpec((1, D), lambda i: (0, 0))
    if w_pre_next is None:
        return pl.pallas_call(
            _post_kernel, grid=(M // tr,),
            in_specs=[row, row, vec], out_specs=row,
            out_shape=jax.ShapeDtypeStruct((M, D), F32),
            compiler_params=_params("parallel"), name="post_norm",
        )(y, x, w_post.reshape(1, D))
    return pl.pallas_call(
        _post_pre_kernel, grid=(M // tr,),
        in_specs=[row, row, vec, vec], out_specs=[row, row],
        out_shape=[jax.ShapeDtypeStruct((M, D), F32),
                   jax.ShapeDtypeStruct((M, D), BF16)],
        compiler_params=_params("parallel"), name="post_pre_norm",
    )(y, x, w_post.reshape(1, D), w_pre_next.reshape(1, D))


def _weight_spec(w, lead, k_rows, tn, index_map):
    assert w.ndim == len(lead) + 2
    return pl.BlockSpec((None,) * len(lead) + (k_rows, tn),
                        lambda *g: tuple(lead) + index_map(*g))


def _mm_kernel(a_ref, w_ref, o_ref):
    o_ref[...] = jnp.dot(a_ref[...], w_ref[...].astype(BF16),
                         preferred_element_type=F32).astype(o_ref.dtype)


def matmul(a, w, out_dtype, tm, tn, lead=(), m_outer=False):
    M, K = a.shape
    N = w.shape[-1]
    assert w.shape[-2] == K and M % tm == 0 and N % tn == 0
    if m_outer:
        grid, ij = (M // tm, N // tn), (lambda i, j: (i, j))
    else:
        grid, ij = (N // tn, M // tm), (lambda j, i: (i, j))
    return pl.pallas_call(
        _mm_kernel,
        grid=grid,
        in_specs=[pl.BlockSpec((tm, K), lambda *g: (ij(*g)[0], 0)),
                  _weight_spec(w, lead, K, tn, lambda *g: (0, ij(*g)[1]))],
        out_specs=pl.BlockSpec((tm, tn), lambda *g: ij(*g)),
        out_shape=jax.ShapeDtypeStruct((M, N), out_dtype),
        compiler_params=_params("parallel", "parallel"),
        name="matmul",
    )(a, w)


def _mm_wt_kernel(a_ref, wt_ref, o_ref):
    o_ref[...] = lax.dot_general(a_ref[...], wt_ref[...].astype(BF16),
                                 (((1,), (1,)), ((), ())),
                                 preferred_element_type=F32).astype(o_ref.dtype)


def matmul_wt(a, wt, out_dtype, tm, tn, row0, n_rows, skip_at=None, skip=0):
    M, K = a.shape
    assert wt.shape[1] == K and M % tm == 0 and n_rows % tn == 0
    assert row0 % SUBLANES == 0 and skip % SUBLANES == 0
    assert row0 + n_rows + skip <= wt.shape[0]
    if skip_at is None:
        skip_at = n_rows
    assert skip_at % tn == 0
    jskip = skip_at // tn

    def w_rows(i, j):
        r = row0 + j * tn + jnp.where(j >= jskip, skip, 0)
        return (pl.multiple_of(r, SUBLANES), 0)

    return pl.pallas_call(
        _mm_wt_kernel,
        grid=(M // tm, n_rows // tn),
        in_specs=[pl.BlockSpec((tm, K), lambda i, j: (i, 0), pipeline_mode=pl.Buffered(1)),
                  pl.BlockSpec((pl.Element(tn), pl.Element(K)), w_rows)],
        out_specs=pl.BlockSpec((tm, tn), lambda i, j: (i, j)),
        out_shape=jax.ShapeDtypeStruct((M, n_rows), out_dtype),
        compiler_params=_params("parallel", "parallel"),
        name="matmul_wt",
    )(a, wt)


def _mm_postnorm_kernel(a_ref, w_ref, x_ref, wpost_ref, wpre_ref, xo_ref, h_ref, y_scr,
                        *, n_m):
    i = pl.program_id(0)
    j = pl.program_id(1)
    n_n, _, tn = y_scr.shape[1:]
    rc, d = x_ref.shape

    def mm():
        y_scr[i % 2, j] = jnp.dot(a_ref[...], w_ref[...].astype(BF16),
                                  preferred_element_type=F32).astype(y_scr.dtype)

    def norm():
        slot = (i + 1) % 2
        cols = [slice(n * tn, (n + 1) * tn) for n in range(n_n)]
        for s in range(rc // NORM_ROWS):
            rs = slice(s * NORM_ROWS, (s + 1) * NORM_ROWS)
            rows = pl.ds(pl.multiple_of(j * rc, rc) + s * NORM_ROWS, NORM_ROWS)
            y_tile = lambda n: y_scr[slot, n, rows, :].astype(F32)
            ssq = None
            for n in range(n_n):
                y = y_tile(n)
                ssq = y * y if ssq is None else ssq + y * y
            r_post = lax.rsqrt(jnp.sum(ssq, axis=-1, keepdims=True) / d + EPS)
            ssq = None
            for n, c in enumerate(cols):
                xn = x_ref[rs, c] + y_tile(n) * r_post * wpost_ref[:, c]
                xo_ref[rs, c] = xn
                ssq = xn * xn if ssq is None else ssq + xn * xn
            r_pre = lax.rsqrt(jnp.sum(ssq, axis=-1, keepdims=True) / d + EPS)
            for c in cols:
                h_ref[rs, c] = (xo_ref[rs, c] * r_pre * wpre_ref[:, c]).astype(h_ref.dtype)

    @pl.when(i == 0)
    def _():
        mm()

    @pl.when((i > 0) & (i < n_m))
    def _():
        norm()
        mm()

    @pl.when(i == n_m)
    def _():
        norm()


def matmul_post_pre_norm(a, w, x, w_post, w_pre, tm, tn):
    M, K = a.shape
    D = w.shape[1]
    n_m, n_n = M // tm, D // tn
    rc = tm // n_n
    assert M % tm == 0 and D % tn == 0 and tm % n_n == 0 and rc % (2 * SUBLANES) == 0
    chunk = pl.BlockSpec(
        (rc, D), lambda i, j: (jnp.maximum(i - 1, 0) * n_n + jnp.where(i == 0, 0, j), 0))
    vec = pl.BlockSpec((1, D), lambda i, j: (0, 0))
    return pl.pallas_call(
        functools.partial(_mm_postnorm_kernel, n_m=n_m),
        grid=(n_m + 1, n_n),
        in_specs=[pl.BlockSpec((tm, K), lambda i, j: (jnp.minimum(i, n_m - 1), 0)),
                  pl.BlockSpec((K, tn), lambda i, j: (0, jnp.where(i == n_m, n_n - 1, j))),
                  chunk, vec, vec],
        out_specs=[chunk, chunk],
        out_shape=[jax.ShapeDtypeStruct((M, D), F32), jax.ShapeDtypeStruct((M, D), BF16)],
        scratch_shapes=[pltpu.VMEM((2, n_n, tm, tn), BF16)],
        compiler_params=_params("arbitrary", "arbitrary"),
        name="matmul_post_pre_norm",
    )(a, w, x, w_post.reshape(1, D), w_pre.reshape(1, D))


def _swiglu_kernel(a_ref, wg_ref, wu_ref, o_ref):
    a = a_ref[...]
    g = jnp.dot(a, wg_ref[...].astype(BF16), preferred_element_type=F32)
    u = jnp.dot(a, wu_ref[...].astype(BF16), preferred_element_type=F32)
    o_ref[...] = (g * jax.nn.sigmoid(g) * u).astype(o_ref.dtype)


def swiglu_up(a, wg, wu, tm, tn, lead=()):
    M, K = a.shape
    N = wg.shape[-1]
    assert M % tm == 0 and N % tn == 0
    wspec = _weight_spec(wg, lead, K, tn, lambda i, j: (0, j))
    return pl.pallas_call(
        _swiglu_kernel,
        grid=(M // tm, N // tn),
        in_specs=[pl.BlockSpec((tm, K), lambda i, j: (i, 0)), wspec, wspec],
        out_specs=pl.BlockSpec((tm, tn), lambda i, j: (i, j)),
        out_shape=jax.ShapeDtypeStruct((M, N), BF16),
        compiler_params=_params("parallel", "parallel"),
        name="swiglu_up",
    )(a, wg, wu)


def _branch_kernel(a0_ref, a1_ref, w0_ref, w1_ref, g0_ref, g1_ref, o_ref):
    y0 = jnp.dot(a0_ref[...], w0_ref[...].astype(BF16), preferred_element_type=F32)
    y1 = jnp.dot(a1_ref[...], w1_ref[...].astype(BF16), preferred_element_type=F32)
    o_ref[...] = (jax.nn.sigmoid(g0_ref[...].astype(F32)) * y0
                  + jax.nn.sigmoid(g1_ref[...].astype(F32)) * y1).astype(o_ref.dtype)


def branch_merge(a0, a1, w, lead, proj, g_col0, tm, tn):
    M, K = a0.shape
    N = w.shape[-1]
    assert M % tm == 0 and N % tn == 0 and g_col0 % tn == 0
    gb0 = g_col0 // tn
    gb1 = (g_col0 + N) // tn
    aspec = pl.BlockSpec((tm, K), lambda j, i: (i, 0))
    return pl.pallas_call(
        _branch_kernel,
        grid=(N // tn, M // tm),
        in_specs=[aspec, aspec,
                  _weight_spec(w, tuple(lead) + (0,), K, tn, lambda j, i: (0, j)),
                  _weight_spec(w, tuple(lead) + (1,), K, tn, lambda j, i: (0, j)),
                  pl.BlockSpec((tm, tn), lambda j, i: (i, gb0 + j)),
                  pl.BlockSpec((tm, tn), lambda j, i: (i, gb1 + j))],
        out_specs=pl.BlockSpec((tm, tn), lambda j, i: (i, j)),
        out_shape=jax.ShapeDtypeStruct((M, N), BF16),
        compiler_params=_params("parallel", "parallel"),
        name="branch_merge",
    )(a0, a1, w, w, proj, proj)


def _mlstm_kernel(bias_ref, q_ref, k_ref, v_ref, o_ref, gt_ref, hn_ref, out_ref,
                  c_ref, n_ref, m_ref):
    H = MLSTM_HEADS
    L = q_ref.shape[0]
    dqk = q_ref.shape[1] // H
    dv = v_ref.shape[1] // H

    @pl.when(pl.program_id(1) == 0)
    def _():
        c_ref[...] = jnp.zeros_like(c_ref)
        n_ref[...] = jnp.zeros_like(n_ref)
        m_ref[...] = jnp.zeros_like(m_ref)

    pre = GATE_CAP * jnp.tanh((gt_ref[...] + bias_ref[...]) / GATE_CAP)
    lf = jnp.minimum(pre, 0.0) - jnp.log1p(jnp.exp(-jnp.abs(pre)))
    lane = lax.broadcasted_iota(jnp.int32, lf.shape, 1)
    b_all = lf
    sh = 1
    while sh < L:
        b_all = b_all + jnp.where(lane >= sh, pltpu.roll(b_all, sh, axis=1), 0.0)
        sh *= 2

    for h in range(H):
        out, c_new, n_new, m_new = _mlstm_head(
            pre[h:h + 1, :], b_all[H + h:H + h + 1, :],
            q_ref[:, h * dqk:(h + 1) * dqk], k_ref[:, h * dqk:(h + 1) * dqk],
            v_ref[:, h * dv:(h + 1) * dv], o_ref[:, h * dv:(h + 1) * dv],
            hn_ref[:, h * dv:(h + 1) * dv], c_ref[h], n_ref[h], m_ref[h][:, 0:1])
        out_ref[:, h * dv:(h + 1) * dv] = out.astype(out_ref.dtype)
        c_ref[h] = c_new
        n_ref[h] = n_new
        m_ref[h] = jnp.broadcast_to(m_new, m_ref.shape[1:])


def _mlstm_head(li, b_row, q, k, v, o_pre, head_norm, c_old, n_old, m_prev):
    L, dqk = q.shape
    scale = dqk ** -0.5
    b_sq = jnp.broadcast_to(b_row, (L, L)).T
    row = lax.broadcasted_iota(jnp.int32, (L, L), 0)
    col = lax.broadcasted_iota(jnp.int32, (L, L), 1)
    dm = jnp.where(col <= row, b_sq - b_row + li, -jnp.inf)
    inter = b_sq[:, 0:1] + m_prev
    m_row = jnp.maximum(inter, jnp.max(dm, axis=-1, keepdims=True))
    w_intra = jnp.exp(dm - m_row)
    s_inter = jnp.exp(inter - m_row)

    qk = lax.dot_general(q, k, (((1,), (1,)), ((), ())),
                         preferred_element_type=F32) * scale * w_intra
    q_c = jnp.dot(q, c_old.astype(BF16), preferred_element_type=F32) * scale
    num = s_inter * q_c + jnp.dot(qk.astype(BF16), v, preferred_element_type=F32)
    q_n = jnp.sum(q.astype(F32) * n_old, axis=-1, keepdims=True) * scale
    den = s_inter * q_n + jnp.sum(qk, axis=-1, keepdims=True)
    hh = num / jnp.maximum(jnp.abs(den), jnp.exp(-m_row))

    b_last = b_row[:, L - 1:L]
    a_row = b_last - b_row + li
    m_new = jnp.maximum(b_last + m_prev, jnp.max(a_row, axis=-1, keepdims=True))
    s_state = jnp.exp(b_last + m_prev - m_new)
    e_col = jnp.broadcast_to(jnp.exp(a_row - m_new), (L, L)).T[:, 0:1]
    kw = k.astype(F32) * e_col
    c_new = s_state * c_old + lax.dot_general(
        kw.astype(BF16), v, (((0,), (0,)), ((), ())), preferred_element_type=F32)
    n_new = s_state * n_old + jnp.sum(kw, axis=0, keepdims=True)

    out = jax.nn.sigmoid(o_pre.astype(F32)) * _rms(hh, head_norm)
    return out, c_new, n_new, m_new


def mlstm_mixer(proj, gates_t, b_if, head_norm, batch, seq, q_col0, k_col0, v_col0, o_col0,
                dqk, dv):
    L = MLSTM_CHUNK
    T = seq // L
    H = MLSTM_HEADS
    qkw, vw = H * dqk, H * dv
    assert q_col0 % qkw == 0 and k_col0 % qkw == 0 and v_col0 % vw == 0 and o_col0 % vw == 0
    qb, kb, vb, ob = q_col0 // qkw, k_col0 // qkw, v_col0 // vw, o_col0 // vw
    row = lambda cb: (lambda b, t: (b * T + t, cb))
    return pl.pallas_call(
        _mlstm_kernel,
        grid=(batch, T),
        in_specs=[
            pl.BlockSpec((2 * H, 1), lambda b, t: (0, 0)),
            pl.BlockSpec((L, qkw), row(qb)),
            pl.BlockSpec((L, qkw), row(kb)),
            pl.BlockSpec((L, vw), row(vb)),
            pl.BlockSpec((L, vw), row(ob)),
            pl.BlockSpec((2 * H, L), lambda b, t: (0, b * T + t)),
            pl.BlockSpec((1, vw), lambda b, t: (0, 0)),
        ],
        out_specs=pl.BlockSpec((L, vw), row(0)),
        out_shape=jax.ShapeDtypeStruct((batch * seq, vw), BF16),
        scratch_shapes=[pltpu.VMEM((H, dqk, dv), F32),
                        pltpu.VMEM((H, 1, dqk), F32),
                        pltpu.VMEM((H, 1, LANES), F32)],
        compiler_params=_params("parallel", "arbitrary"),
        name="mlstm",
    )(b_if.reshape(2 * H, 1), proj, proj, proj, proj, gates_t, head_norm.reshape(1, vw))


def _conv_kernel(cb_ref, cc_ref, cx_ref, w_ref, o_ref, carry_ref):
    s = pl.program_id(2)
    ts = cb_ref.shape[0]

    @pl.when(s == 0)
    def _():
        carry_ref[...] = jnp.zeros_like(carry_ref)

    w = w_ref[...]
    w0, w1, w2 = w[0:1, :], w[1:2, :], w[2:3, :]
    u = cc_ref[...].astype(F32) * cx_ref[...].astype(F32)
    y = w2 * u + w1 * pltpu.roll(u, 1, axis=0) + w0 * pltpu.roll(u, 2, axis=0)
    o_ref[...] = (cb_ref[...].astype(F32) * y).astype(o_ref.dtype)

    prev = carry_ref[...]
    u_h = u[0:8, :]
    r = lax.broadcasted_iota(jnp.int32, u_h.shape, 0)
    u1 = jnp.where(r < 1, pltpu.roll(prev, 1, axis=0), pltpu.roll(u_h, 1, axis=0))
    u2 = jnp.where(r < 2, pltpu.roll(prev, 2, axis=0), pltpu.roll(u_h, 2, axis=0))
    y_h = w2 * u_h + w1 * u1 + w0 * u2
    o_ref[0:8, :] = (cb_ref[0:8, :].astype(F32) * y_h).astype(o_ref.dtype)
    carry_ref[...] = u[ts - 8:ts, :]


def gated_conv(proj, conv_w, batch, seq, cb_col0, cc_col0, cx_col0, ts=512, tc=512):
    C = conv_w.shape[1]
    nS = seq // ts
    b0, c0, x0 = cb_col0 // tc, cc_col0 // tc, cx_col0 // tc
    return pl.pallas_call(
        _conv_kernel,
        grid=(batch, C // tc, nS),
        in_specs=[
            pl.BlockSpec((ts, tc), lambda b, c, s: (b * nS + s, b0 + c)),
            pl.BlockSpec((ts, tc), lambda b, c, s: (b * nS + s, c0 + c)),
            pl.BlockSpec((ts, tc), lambda b, c, s: (b * nS + s, x0 + c)),
            pl.BlockSpec((CONV_K, tc), lambda b, c, s: (0, c)),
        ],
        out_specs=pl.BlockSpec((ts, tc), lambda b, c, s: (b * nS + s, c)),
        out_shape=jax.ShapeDtypeStruct((batch * seq, C), BF16),
        scratch_shapes=[pltpu.VMEM((8, tc), F32)],
        compiler_params=_params("parallel", "parallel", "arbitrary"),
        name="gated_conv",
    )(proj, proj, proj, conv_w)


def _xq_attn_kernel(a_ref, wq_ref, k_ref, v_ref, o_ref, s_scr, *, parts):
    part = pl.program_id(1) % parts
    dh = v_ref.shape[1]
    q = jnp.dot(a_ref[...], wq_ref[...].astype(BF16),
                preferred_element_type=F32).astype(BF16)
    s_part = lax.dot_general(q, k_ref[...], (((1,), (1,)), ((), ())),
                             preferred_element_type=F32)

    @pl.when(part == 0)
    def _():
        s_scr[...] = s_part

    @pl.when((part > 0) & (part < parts - 1))
    def _():
        s_scr[...] += s_part

    @pl.when(part == parts - 1)
    def _():
        s = (s_scr[...] + s_part) * (dh ** -0.5)
        e = jnp.exp(s - jnp.max(s, axis=-1, keepdims=True))
        p = e / jnp.sum(e, axis=-1, keepdims=True)
        o_ref[...] = jnp.dot(p.astype(BF16), v_ref[...],
                             preferred_element_type=F32).astype(o_ref.dtype)


def xq_cross_attention(a, w_xq, lead, k, v, seq, mem_len, dh, tm, tn):
    M, K = a.shape
    H = XATTN_HEADS
    parts = dh // tn
    assert M % tm == 0 and seq % tm == 0 and dh % tn == 0 and parts >= 2
    tiles_per_batch = seq // tm
    return pl.pallas_call(
        functools.partial(_xq_attn_kernel, parts=parts),
        grid=(M // tm, H * parts),
        in_specs=[
            pl.BlockSpec((tm, K), lambda i, j: (i, 0)),
            _weight_spec(w_xq, lead, K, tn, lambda i, j: (0, j)),
            pl.BlockSpec((mem_len, tn), lambda i, j: (i // tiles_per_batch, j)),
            pl.BlockSpec((mem_len, dh), lambda i, j: (i // tiles_per_batch, j // parts)),
        ],
        out_specs=pl.BlockSpec((tm, dh), lambda i, j: (i, j // parts)),
        out_shape=jax.ShapeDtypeStruct((M, H * dh), BF16),
        scratch_shapes=[pltpu.VMEM((tm, mem_len), F32)],
        compiler_params=_params("parallel", "arbitrary"),
        name="xq_cross_attention",
    )(a, w_xq, k, v)


def kernel(x, mem, norm_pre_mix, norm_post_mix, w_in, b_if, mlstm_head_norm, conv_w, w_branch, w_mix_out, norm_pre_xattn, norm_post_xattn, norm_mem, w_xq, w_xk, w_xv, w_xo, norm_pre_ffn, norm_post_ffn, w_ffn_gate, w_ffn_up, w_ffn_down):
    B, S, D = x.shape
    mem_len = mem.shape[1]
    depth = w_in.shape[0]
    H = MLSTM_HEADS
    v_width = mlstm_head_norm.shape[1]
    conv_width = conv_w.shape[2]
    dv = v_width // H
    dqk = dv // 2
    qk_width = H * dqk
    dh = w_xq.shape[2] // XATTN_HEADS

    in_width = w_in.shape[2]
    w_in_t = jnp.swapaxes(w_in, 1, 2).reshape(depth * in_width, D)
    q0 = 0
    k0 = q0 + qk_width
    v0 = k0 + qk_width
    o0 = v0 + v_width
    gate0 = o0 + v_width
    cb0 = gate0
    cc0 = cb0 + conv_width
    cx0 = cc0 + conv_width
    g0 = cx0 + conv_width

    xr = x.reshape(B * S, D)
    memr = mem.reshape(B * mem_len, D)
    for l in range(depth):
        h = prenorm(xr, norm_pre_mix[l])
        proj = matmul_wt(h, w_in_t, BF16, 1024, 1024, l * in_width, in_width - 2 * H,
                         skip_at=gate0, skip=2 * H)
        gates_t = matmul_wt(h, w_in_t, F32, 1024, 2 * H, l * in_width + gate0, 2 * H).T
        h_a = mlstm_mixer(proj, gates_t, b_if[l], mlstm_head_norm[l], B, S,
                          q0, k0, v0, o0, dqk, dv)
        h_b = gated_conv(proj, conv_w[l], B, S, cb0, cc0, cx0, ts=1024, tc=1024)
        merged = branch_merge(h_a, h_b, w_branch, (l,), proj, g0, 1024, 512)
        xr, h = matmul_post_pre_norm(merged, w_mix_out[l].astype(BF16), xr,
                                     norm_post_mix[l], norm_pre_xattn[l], 1024, 512)

        m = prenorm(memr, norm_mem[l])
        k = matmul(m, w_xk, BF16, B * mem_len, 512, lead=(l,))
        v = matmul(m, w_xv, BF16, B * mem_len, 512, lead=(l,))
        o = xq_cross_attention(h, w_xq, (l,), k, v, S, mem_len, dh, 1024, 512)
        xr, h = matmul_post_pre_norm(o, w_xo[l].astype(BF16), xr,
                                     norm_post_xattn[l], norm_pre_ffn[l], 1024, 512)

        hidden = swiglu_up(h, w_ffn_gate, w_ffn_up, 2048, 256, lead=(l,))
        y = matmul(hidden, w_ffn_down[l].astype(BF16), BF16, 512, 512, m_outer=True)
        xr = post_norm_residual(y, xr, norm_post_ffn[l])
    return xr.reshape(B, S, D)
```

```python
import functools

import jax
import jax.numpy as jnp
from jax import lax
from jax.experimental import pallas as pl
from jax.experimental.pallas import tpu as pltpu

F32 = jnp.float32
BF16 = jnp.bfloat16

MLSTM_HEADS = 4
XATTN_HEADS = 4
CONV_K = 3
GATE_CAP = 15.0
EPS = 1e-6
MLSTM_CHUNK = 256
LANES = 128
SUBLANES = 8
NORM_ROWS = 32
EPILOGUE_ROWS = 256

VMEM_LIMIT_BYTES = 56 * 1024 * 1024


def _params(*sem):
    return pltpu.CompilerParams(dimension_semantics=sem,
                                vmem_limit_bytes=VMEM_LIMIT_BYTES)


def _ceil_div(a, b):
    return -(-a // b)


def _rms(x, w):
    return x * lax.rsqrt(jnp.mean(x * x, axis=-1, keepdims=True) + EPS) * w


def _prenorm_kernel(x_ref, w_ref, h_ref):
    h_ref[...] = _rms(x_ref[...].astype(F32), w_ref[...]).astype(h_ref.dtype)


def prenorm(x, w, tr=256):
    M, D = x.shape
    return pl.pallas_call(
        _prenorm_kernel,
        grid=(M // tr,),
        in_specs=[pl.BlockSpec((tr, D), lambda i: (i, 0)),
                  pl.BlockSpec((1, D), lambda i: (0, 0))],
        out_specs=pl.BlockSpec((tr, D), lambda i: (i, 0)),
        out_shape=jax.ShapeDtypeStruct((M, D), BF16),
        compiler_params=_params("parallel"),
        name="prenorm",
    )(x, w.reshape(1, D))


def _post_pre_kernel(y_ref, x_ref, wpost_ref, wpre_ref, xo_ref, h_ref):
    xn = x_ref[...] + _rms(y_ref[...].astype(F32), wpost_ref[...])
    xo_ref[...] = xn
    h_ref[...] = _rms(xn, wpre_ref[...]).astype(h_ref.dtype)


def _post_kernel(y_ref, x_ref, wpost_ref, xo_ref):
    xo_ref[...] = x_ref[...] + _rms(y_ref[...].astype(F32), wpost_ref[...])


def post_norm_residual(y, x, w_post, w_pre_next=None, tr=256):
    M, D = x.shape
    row = pl.BlockSpec((tr, D), lambda i: (i, 0))
    vec = pl.BlockSpec((1, D), lambda i: (0, 0))
    if w_pre_next is None:
        return pl.pallas_call(
            _post_kernel, grid=(M // tr,),
            in_specs=[row, row, vec], out_specs=row,
            out_shape=jax.ShapeDtypeStruct((M, D), F32),
            compiler_params=_params("parallel"), name="post_norm",
        )(y, x, w_post.reshape(1, D))
    return pl.pallas_call(
        _post_pre_kernel, grid=(M // tr,),
        in_specs=[row, row, vec, vec], out_specs=[row, row],
        out_shape=[jax.ShapeDtypeStruct((M, D), F32),
                   jax.ShapeDtypeStruct((M, D), BF16)],
        compiler_params=_params("parallel"), name="post_pre_norm",
    )(y, x, w_post.reshape(1, D), w_pre_next.reshape(1, D))


def _weight_spec(w, lead, k_rows, tn, index_map):
    assert w.ndim == len(lead) + 2
    return pl.BlockSpec((None,) * len(lead) + (k_rows, tn),
                        lambda *g: tuple(lead) + index_map(*g))


def _mm_kernel(a_ref, w_ref, o_ref):
    o_ref[...] = jnp.dot(a_ref[...], w_ref[...].astype(BF16),
                         preferred_element_type=F32).astype(o_ref.dtype)


def matmul(a, w, out_dtype, tm, tn, lead=(), m_outer=False):
    M, K = a.shape
    N = w.shape[-1]
    assert w.shape[-2] == K and M % tm == 0 and N % tn == 0
    if m_outer:
        grid, ij = (M // tm, N // tn), (lambda i, j: (i, j))
    else:
        grid, ij = (N // tn, M // tm), (lambda j, i: (i, j))
    return pl.pallas_call(
        _mm_kernel,
        grid=grid,
        in_specs=[pl.BlockSpec((tm, K), lambda *g: (ij(*g)[0], 0)),
                  _weight_spec(w, lead, K, tn, lambda *g: (0, ij(*g)[1]))],
        out_specs=pl.BlockSpec((tm, tn), lambda *g: ij(*g)),
        out_shape=jax.ShapeDtypeStruct((M, N), out_dtype),
        compiler_params=_params("parallel", "parallel"),
        name="matmul",
    )(a, w)


def _side_cast_specs(src, lead, n_steps, step_of):
    assert src.ndim == len(lead) + 2
    R, C = src.shape[-2:]
    pack = 2 * SUBLANES
    rows = _ceil_div(_ceil_div(R, n_steps), pack) * pack
    n_blk = _ceil_div(R, rows)
    idx = lambda *g: (jnp.minimum(step_of(*g), n_blk - 1), 0)
    in_spec = pl.BlockSpec((None,) * len(lead) + (rows, C),
                           lambda *g: tuple(lead) + idx(*g))
    return in_spec, pl.BlockSpec((rows, C), idx), jax.ShapeDtypeStruct((R, C), BF16)


def _in_proj_kernel(a_ref, wt_ref, wgate_ref, src_ref, o_ref, gates_ref, dst_ref):
    nt = (((1,), (1,)), ((), ()))
    a = a_ref[...]
    o_ref[...] = lax.dot_general(a, wt_ref[...].astype(BF16), nt,
                                 preferred_element_type=F32).astype(o_ref.dtype)

    @pl.when(pl.program_id(1) == 0)
    def _():
        gates_ref[...] = lax.dot_general(a, wgate_ref[...].astype(BF16), nt,
                                         preferred_element_type=F32)

    dst_ref[...] = src_ref[...].astype(dst_ref.dtype)


def input_projection(a, wt, row0, gate0, n_gate, n_cols, tm, tn, side_src, side_lead):
    M, K = a.shape
    assert wt.shape[1] == K and M % tm == 0 and n_cols % tn == 0 and gate0 % tn == 0
    assert row0 % SUBLANES == 0 and n_gate % SUBLANES == 0
    assert row0 + n_cols + n_gate <= wt.shape[0]
    n_m, n_n = M // tm, n_cols // tn
    jgate = gate0 // tn

    def w_rows(i, j):
        r = row0 + j * tn + jnp.where(j >= jgate, n_gate, 0)
        return (pl.multiple_of(r, SUBLANES), 0)

    side_in, side_out, side_shape = _side_cast_specs(
        side_src, side_lead, n_m * n_n, lambda i, j: i * n_n + j)
    return pl.pallas_call(
        _in_proj_kernel,
        grid=(n_m, n_n),
        in_specs=[pl.BlockSpec((tm, K), lambda i, j: (i, 0), pipeline_mode=pl.Buffered(1)),
                  pl.BlockSpec((pl.Element(tn), pl.Element(K)), w_rows),
                  pl.BlockSpec((pl.Element(n_gate), pl.Element(K)),
                               lambda i, j: (row0 + gate0, 0)),
                  side_in],
        out_specs=[pl.BlockSpec((tm, tn), lambda i, j: (i, j)),
                   pl.BlockSpec((tm, n_gate), lambda i, j: (i, 0)),
                   side_out],
        out_shape=[jax.ShapeDtypeStruct((M, n_cols), BF16),
                   jax.ShapeDtypeStruct((M, n_gate), F32),
                   side_shape],
        compiler_params=_params("arbitrary", "arbitrary"),
        name="input_projection",
    )(a, wt, wt, side_src)


def _mm_postnorm_kernel(a_ref, w_ref, x_ref, wpost_ref, wpre_ref, xo_ref, h_ref, y_scr,
                        *, n_m):
    i = pl.program_id(0)
    j = pl.program_id(1)
    n_n, _, tn = y_scr.shape[1:]
    rc, d = x_ref.shape

    def mm():
        y_scr[i % 2, j] = jnp.dot(a_ref[...], w_ref[...].astype(BF16),
                                  preferred_element_type=F32).astype(y_scr.dtype)

    def norm():
        slot = (i + 1) % 2
        cols = [slice(n * tn, (n + 1) * tn) for n in range(n_n)]
        for s in range(rc // NORM_ROWS):
            rs = slice(s * NORM_ROWS, (s + 1) * NORM_ROWS)
            rows = pl.ds(pl.multiple_of(j * rc, rc) + s * NORM_ROWS, NORM_ROWS)
            y_tile = lambda n: y_scr[slot, n, rows, :].astype(F32)
            ssq = None
            for n in range(n_n):
                y = y_tile(n)
                ssq = y * y if ssq is None else ssq + y * y
            r_post = lax.rsqrt(jnp.sum(ssq, axis=-1, keepdims=True) / d + EPS)
            ssq = None
            for n, c in enumerate(cols):
                xn = x_ref[rs, c] + y_tile(n) * r_post * wpost_ref[:, c]
                xo_ref[rs, c] = xn
                ssq = xn * xn if ssq is None else ssq + xn * xn
            r_pre = lax.rsqrt(jnp.sum(ssq, axis=-1, keepdims=True) / d + EPS)
            for c in cols:
                h_ref[rs, c] = (xo_ref[rs, c] * r_pre * wpre_ref[:, c]).astype(h_ref.dtype)

    @pl.when(i == 0)
    def _():
        mm()

    @pl.when((i > 0) & (i < n_m))
    def _():
        norm()
        mm()

    @pl.when(i == n_m)
    def _():
        norm()


def matmul_post_pre_norm(a, w, x, w_post, w_pre, tm, tn):
    M, K = a.shape
    D = w.shape[1]
    n_m, n_n = M // tm, D // tn
    rc = tm // n_n
    assert M % tm == 0 and D % tn == 0 and tm % n_n == 0 and rc % (2 * SUBLANES) == 0
    chunk = pl.BlockSpec(
        (rc, D), lambda i, j: (jnp.maximum(i - 1, 0) * n_n + jnp.where(i == 0, 0, j), 0))
    vec = pl.BlockSpec((1, D), lambda i, j: (0, 0))
    return pl.pallas_call(
        functools.partial(_mm_postnorm_kernel, n_m=n_m),
        grid=(n_m + 1, n_n),
        in_specs=[pl.BlockSpec((tm, K), lambda i, j: (jnp.minimum(i, n_m - 1), 0)),
                  pl.BlockSpec((K, tn), lambda i, j: (0, jnp.where(i == n_m, n_n - 1, j))),
                  chunk, vec, vec],
        out_specs=[chunk, chunk],
        out_shape=[jax.ShapeDtypeStruct((M, D), F32), jax.ShapeDtypeStruct((M, D), BF16)],
        scratch_shapes=[pltpu.VMEM((2, n_n, tm, tn), BF16)],
        compiler_params=_params("arbitrary", "arbitrary"),
        name="matmul_post_pre_norm",
    )(a, w, x, w_post.reshape(1, D), w_pre.reshape(1, D))


def _swiglu_kernel(a_ref, wg_ref, wu_ref, src_ref, o_ref, dst_ref):
    dst_ref[...] = src_ref[...].astype(dst_ref.dtype)
    wg = wg_ref[...].astype(BF16)
    wu = wu_ref[...].astype(BF16)
    for r0 in range(0, o_ref.shape[0], EPILOGUE_ROWS):
        rs = slice(r0, r0 + EPILOGUE_ROWS)
        a = a_ref[rs, :]
        g = jnp.dot(a, wg, preferred_element_type=F32)
        u = jnp.dot(a, wu, preferred_element_type=F32)
        o_ref[rs, :] = (g * jax.nn.sigmoid(g) * u).astype(o_ref.dtype)


def swiglu_up(a, wg, wu, tm, tn, lead, side_src):
    M, K = a.shape
    N = wg.shape[-1]
    assert M % tm == 0 and N % tn == 0
    n_m, n_n = M // tm, N // tn
    wspec = _weight_spec(wg, lead, K, tn, lambda i, j: (0, j))
    side_in, side_out, side_shape = _side_cast_specs(
        side_src, lead, n_m * n_n, lambda i, j: i * n_n + j)
    return pl.pallas_call(
        _swiglu_kernel,
        grid=(n_m, n_n),
        in_specs=[pl.BlockSpec((tm, K), lambda i, j: (i, 0), pipeline_mode=pl.Buffered(1)),
                  wspec, wspec, side_in],
        out_specs=[pl.BlockSpec((tm, tn), lambda i, j: (i, j)), side_out],
        out_shape=[jax.ShapeDtypeStruct((M, N), BF16), side_shape],
        compiler_params=_params("arbitrary", "arbitrary"),
        name="swiglu_up",
    )(a, wg, wu, side_src)


def _branch_kernel(a0_ref, a1_ref, w0_ref, w1_ref, g0_ref, g1_ref, o_ref):
    w0 = w0_ref[...].astype(BF16)
    w1 = w1_ref[...].astype(BF16)
    for r0 in range(0, o_ref.shape[0], EPILOGUE_ROWS):
        rs = slice(r0, r0 + EPILOGUE_ROWS)
        y0 = jnp.dot(a0_ref[rs, :], w0, preferred_element_type=F32)
        y1 = jnp.dot(a1_ref[rs, :], w1, preferred_element_type=F32)
        o_ref[rs, :] = (jax.nn.sigmoid(g0_ref[rs, :].astype(F32)) * y0
                        + jax.nn.sigmoid(g1_ref[rs, :].astype(F32)) * y1).astype(o_ref.dtype)


def branch_merge(a0, a1, w, lead, proj, g_col0, tm, tn):
    M, K = a0.shape
    N = w.shape[-1]
    assert M % tm == 0 and N % tn == 0 and g_col0 % tn == 0
    gb0 = g_col0 // tn
    gb1 = (g_col0 + N) // tn
    aspec = pl.BlockSpec((tm, K), lambda j, i: (i, 0))
    return pl.pallas_call(
        _branch_kernel,
        grid=(N // tn, M // tm),
        in_specs=[aspec, aspec,
                  _weight_spec(w, tuple(lead) + (0,), K, tn, lambda j, i: (0, j)),
                  _weight_spec(w, tuple(lead) + (1,), K, tn, lambda j, i: (0, j)),
                  pl.BlockSpec((tm, tn), lambda j, i: (i, gb0 + j)),
                  pl.BlockSpec((tm, tn), lambda j, i: (i, gb1 + j))],
        out_specs=pl.BlockSpec((tm, tn), lambda j, i: (i, j)),
        out_shape=jax.ShapeDtypeStruct((M, N), BF16),
        compiler_params=_params("parallel", "parallel"),
        name="branch_merge",
    )(a0, a1, w, w, proj, proj)


def _mlstm_kernel(bias_ref, q_ref, k_ref, v_ref, o_ref, gt_ref, hn_ref, out_ref,
                  c_ref, n_ref, m_ref):
    H = MLSTM_HEADS
    L = q_ref.shape[0]
    dqk = q_ref.shape[1] // H
    dv = v_ref.shape[1] // H

    @pl.when(pl.program_id(1) == 0)
    def _():
        c_ref[...] = jnp.zeros_like(c_ref)
        n_ref[...] = jnp.zeros_like(n_ref)
        m_ref[...] = jnp.zeros_like(m_ref)

    pre = GATE_CAP * jnp.tanh((gt_ref[...] + bias_ref[...]) / GATE_CAP)
    lf = jnp.minimum(pre, 0.0) - jnp.log1p(jnp.exp(-jnp.abs(pre)))
    lane = lax.broadcasted_iota(jnp.int32, lf.shape, 1)
    b_all = lf
    sh = 1
    while sh < L:
        b_all = b_all + jnp.where(lane >= sh, pltpu.roll(b_all, sh, axis=1), 0.0)
        sh *= 2

    for h in range(H):
        out, c_new, n_new, m_new = _mlstm_head(
            pre[h:h + 1, :], b_all[H + h:H + h + 1, :],
            q_ref[:, h * dqk:(h + 1) * dqk], k_ref[:, h * dqk:(h + 1) * dqk],
            v_ref[:, h * dv:(h + 1) * dv], o_ref[:, h * dv:(h + 1) * dv],
            hn_ref[:, h * dv:(h + 1) * dv], c_ref[h], n_ref[h], m_ref[h][:, 0:1])
        out_ref[:, h * dv:(h + 1) * dv] = out.astype(out_ref.dtype)
        c_ref[h] = c_new
        n_ref[h] = n_new
        m_ref[h] = jnp.broadcast_to(m_new, m_ref.shape[1:])


def _mlstm_head(li, b_row, q, k, v, o_pre, head_norm, c_old, n_old, m_prev):
    L, dqk = q.shape
    scale = dqk ** -0.5
    b_sq = jnp.broadcast_to(b_row, (L, L)).T
    row = lax.broadcasted_iota(jnp.int32, (L, L), 0)
    col = lax.broadcasted_iota(jnp.int32, (L, L), 1)
    dm = jnp.where(col <= row, b_sq - b_row + li, -jnp.inf)
    inter = b_sq[:, 0:1] + m_prev
    m_row = jnp.maximum(inter, jnp.max(dm, axis=-1, keepdims=True))
    w_intra = jnp.exp(dm - m_row)
    s_inter = jnp.exp(inter - m_row)

    qk = lax.dot_general(q, k, (((1,), (1,)), ((), ())),
                         preferred_element_type=F32) * scale * w_intra
    q_c = jnp.dot(q, c_old.astype(BF16), preferred_element_type=F32) * scale
    num = s_inter * q_c + jnp.dot(qk.astype(BF16), v, preferred_element_type=F32)
    q_n = jnp.sum(q.astype(F32) * n_old, axis=-1, keepdims=True) * scale
    den = s_inter * q_n + jnp.sum(qk, axis=-1, keepdims=True)
    hh = num / jnp.maximum(jnp.abs(den), jnp.exp(-m_row))

    b_last = b_row[:, L - 1:L]
    a_row = b_last - b_row + li
    m_new = jnp.maximum(b_last + m_prev, jnp.max(a_row, axis=-1, keepdims=True))
    s_state = jnp.exp(b_last + m_prev - m_new)
    e_col = jnp.broadcast_to(jnp.exp(a_row - m_new), (L, L)).T[:, 0:1]
    kw = k.astype(F32) * e_col
    c_new = s_state * c_old + lax.dot_general(
        kw.astype(BF16), v, (((0,), (0,)), ((), ())), preferred_element_type=F32)
    n_new = s_state * n_old + jnp.sum(kw, axis=0, keepdims=True)

    out = jax.nn.sigmoid(o_pre.astype(F32)) * _rms(hh, head_norm)
    return out, c_new, n_new, m_new


def mlstm_mixer(proj, gates_t, b_if, head_norm, batch, seq, q_col0, k_col0, v_col0, o_col0,
                dqk, dv):
    L = MLSTM_CHUNK
    T = seq // L
    H = MLSTM_HEADS
    qkw, vw = H * dqk, H * dv
    assert q_col0 % qkw == 0 and k_col0 % qkw == 0 and v_col0 % vw == 0 and o_col0 % vw == 0
    qb, kb, vb, ob = q_col0 // qkw, k_col0 // qkw, v_col0 // vw, o_col0 // vw
    row = lambda cb: (lambda b, t: (b * T + t, cb))
    return pl.pallas_call(
        _mlstm_kernel,
        grid=(batch, T),
        in_specs=[
            pl.BlockSpec((2 * H, 1), lambda b, t: (0, 0)),
            pl.BlockSpec((L, qkw), row(qb)),
            pl.BlockSpec((L, qkw), row(kb)),
            pl.BlockSpec((L, vw), row(vb)),
            pl.BlockSpec((L, vw), row(ob)),
            pl.BlockSpec((2 * H, L), lambda b, t: (0, b * T + t)),
            pl.BlockSpec((1, vw), lambda b, t: (0, 0)),
        ],
        out_specs=pl.BlockSpec((L, vw), row(0)),
        out_shape=jax.ShapeDtypeStruct((batch * seq, vw), BF16),
        scratch_shapes=[pltpu.VMEM((H, dqk, dv), F32),
                        pltpu.VMEM((H, 1, dqk), F32),
                        pltpu.VMEM((H, 1, LANES), F32)],
        compiler_params=_params("parallel", "arbitrary"),
        name="mlstm",
    )(b_if.reshape(2 * H, 1), proj, proj, proj, proj, gates_t, head_norm.reshape(1, vw))


def _conv_kernel(cb_ref, cc_ref, cx_ref, w_ref, o_ref, carry_ref):
    s = pl.program_id(2)
    ts = cb_ref.shape[0]

    @pl.when(s == 0)
    def _():
        carry_ref[...] = jnp.zeros_like(carry_ref)

    w = w_ref[...]
    w0, w1, w2 = w[0:1, :], w[1:2, :], w[2:3, :]
    u = cc_ref[...].astype(F32) * cx_ref[...].astype(F32)
    y = w2 * u + w1 * pltpu.roll(u, 1, axis=0) + w0 * pltpu.roll(u, 2, axis=0)
    o_ref[...] = (cb_ref[...].astype(F32) * y).astype(o_ref.dtype)

    prev = carry_ref[...]
    u_h = u[0:8, :]
    r = lax.broadcasted_iota(jnp.int32, u_h.shape, 0)
    u1 = jnp.where(r < 1, pltpu.roll(prev, 1, axis=0), pltpu.roll(u_h, 1, axis=0))
    u2 = jnp.where(r < 2, pltpu.roll(prev, 2, axis=0), pltpu.roll(u_h, 2, axis=0))
    y_h = w2 * u_h + w1 * u1 + w0 * u2
    o_ref[0:8, :] = (cb_ref[0:8, :].astype(F32) * y_h).astype(o_ref.dtype)
    carry_ref[...] = u[ts - 8:ts, :]


def gated_conv(proj, conv_w, batch, seq, cb_col0, cc_col0, cx_col0, ts=512, tc=512):
    C = conv_w.shape[1]
    nS = seq // ts
    b0, c0, x0 = cb_col0 // tc, cc_col0 // tc, cx_col0 // tc
    return pl.pallas_call(
        _conv_kernel,
        grid=(batch, C // tc, nS),
        in_specs=[
            pl.BlockSpec((ts, tc), lambda b, c, s: (b * nS + s, b0 + c)),
            pl.BlockSpec((ts, tc), lambda b, c, s: (b * nS + s, c0 + c)),
            pl.BlockSpec((ts, tc), lambda b, c, s: (b * nS + s, x0 + c)),
            pl.BlockSpec((CONV_K, tc), lambda b, c, s: (0, c)),
        ],
        out_specs=pl.BlockSpec((ts, tc), lambda b, c, s: (b * nS + s, c)),
        out_shape=jax.ShapeDtypeStruct((batch * seq, C), BF16),
        scratch_shapes=[pltpu.VMEM((8, tc), F32)],
        compiler_params=_params("parallel", "parallel", "arbitrary"),
        name="gated_conv",
    )(proj, proj, proj, conv_w)


def _xq_attn_kernel(a_ref, wq_ref, k_ref, v_ref, src_ref, o_ref, dst_ref, s_scr, *, parts):
    dst_ref[...] = src_ref[...].astype(dst_ref.dtype)
    part = pl.program_id(1) % parts
    dh = v_ref.shape[1]
    q = jnp.dot(a_ref[...], wq_ref[...].astype(BF16),
                preferred_element_type=F32).astype(BF16)
    s_part = lax.dot_general(q, k_ref[...], (((1,), (1,)), ((), ())),
                             preferred_element_type=F32)

    @pl.when(part == 0)
    def _():
        s_scr[...] = s_part

    @pl.when((part > 0) & (part < parts - 1))
    def _():
        s_scr[...] += s_part

    @pl.when(part == parts - 1)
    def _():
        s = (s_scr[...] + s_part) * (dh ** -0.5)
        e = jnp.exp(s - jnp.max(s, axis=-1, keepdims=True))
        p = e / jnp.sum(e, axis=-1, keepdims=True)
        o_ref[...] = jnp.dot(p.astype(BF16), v_ref[...],
                             preferred_element_type=F32).astype(o_ref.dtype)


def xq_cross_attention(a, w_xq, lead, k, v, seq, mem_len, dh, tm, tn, side_src):
    M, K = a.shape
    H = XATTN_HEADS
    parts = dh // tn
    assert M % tm == 0 and seq % tm == 0 and dh % tn == 0 and parts >= 2
    tiles_per_batch = seq // tm
    n_m, n_n = M // tm, H * parts
    side_in, side_out, side_shape = _side_cast_specs(
        side_src, lead, n_m * n_n, lambda i, j: i * n_n + j)
    return pl.pallas_call(
        functools.partial(_xq_attn_kernel, parts=parts),
        grid=(n_m, n_n),
        in_specs=[
            pl.BlockSpec((tm, K), lambda i, j: (i, 0)),
            _weight_spec(w_xq, lead, K, tn, lambda i, j: (0, j)),
            pl.BlockSpec((mem_len, tn), lambda i, j: (i // tiles_per_batch, j)),
            pl.BlockSpec((mem_len, dh), lambda i, j: (i // tiles_per_batch, j // parts)),
            side_in,
        ],
        out_specs=[pl.BlockSpec((tm, dh), lambda i, j: (i, j // parts)), side_out],
        out_shape=[jax.ShapeDtypeStruct((M, H * dh), BF16), side_shape],
        scratch_shapes=[pltpu.VMEM((tm, mem_len), F32)],
        compiler_params=_params("arbitrary", "arbitrary"),
        name="xq_cross_attention",
    )(a, w_xq, k, v, side_src)


def kernel(x, mem, norm_pre_mix, norm_post_mix, w_in, b_if, mlstm_head_norm, conv_w, w_branch, w_mix_out, norm_pre_xattn, norm_post_xattn, norm_mem, w_xq, w_xk, w_xv, w_xo, norm_pre_ffn, norm_post_ffn, w_ffn_gate, w_ffn_up, w_ffn_down):
    B, S, D = x.shape
    mem_len = mem.shape[1]
    depth = w_in.shape[0]
    H = MLSTM_HEADS
    v_width = mlstm_head_norm.shape[1]
    conv_width = conv_w.shape[2]
    dv = v_width // H
    dqk = dv // 2
    qk_width = H * dqk
    dh = w_xq.shape[2] // XATTN_HEADS

    in_width = w_in.shape[2]
    w_in_t = jnp.swapaxes(w_in, 1, 2).reshape(depth * in_width, D)
    q0 = 0
    k0 = q0 + qk_width
    v0 = k0 + qk_width
    o0 = v0 + v_width
    gate0 = o0 + v_width
    cb0 = gate0
    cc0 = cb0 + conv_width
    cx0 = cc0 + conv_width
    g0 = cx0 + conv_width

    xr = x.reshape(B * S, D)
    memr = mem.reshape(B * mem_len, D)
    for l in range(depth):
        h = prenorm(xr, norm_pre_mix[l])
        proj, gates, w_mix_bf = input_projection(
            h, w_in_t, l * in_width, gate0, 2 * H, in_width - 2 * H, 1024, 1024,
            w_mix_out, (l,))
        h_a = mlstm_mixer(proj, gates.T, b_if[l], mlstm_head_norm[l], B, S,
                          q0, k0, v0, o0, dqk, dv)
        h_b = gated_conv(proj, conv_w[l], B, S, cb0, cc0, cx0, ts=1024, tc=1024)
        merged = branch_merge(h_a, h_b, w_branch, (l,), proj, g0, 1024, 512)
        xr, h = matmul_post_pre_norm(merged, w_mix_bf, xr,
                                     norm_post_mix[l], norm_pre_xattn[l], 1024, 512)

        m = prenorm(memr, norm_mem[l])
        k = matmul(m, w_xk, BF16, B * mem_len, 512, lead=(l,))
        v = matmul(m, w_xv, BF16, B * mem_len, 512, lead=(l,))
        o, w_xo_bf = xq_cross_attention(h, w_xq, (l,), k, v, S, mem_len, dh, 1024, 512,
                                        w_xo)
        xr, h = matmul_post_pre_norm(o, w_xo_bf, xr,
                                     norm_post_xattn[l], norm_pre_ffn[l], 1024, 512)

        hidden, w_down_bf = swiglu_up(h, w_ffn_gate, w_ffn_up, 2048, 256, (l,), w_ffn_down)
        y = matmul(hidden, w_down_bf, BF16, 512, 512, m_outer=True)
        xr = post_norm_residual(y, xr, norm_post_ffn[l])
    return xr.reshape(B, S, D)
```

```python
import functools

import jax
import jax.numpy as jnp
from jax import lax
from jax.experimental import pallas as pl
from jax.experimental.pallas import tpu as pltpu

F32 = jnp.float32
BF16 = jnp.bfloat16

MLSTM_HEADS = 4
XATTN_HEADS = 4
CONV_K = 3
GATE_CAP = 15.0
EPS = 1e-6
MLSTM_CHUNK = 256
LANES = 128
SUBLANES = 8
NORM_ROWS = 32
EPILOGUE_ROWS = 256

VMEM_LIMIT_BYTES = 56 * 1024 * 1024


def _params(*sem):
    return pltpu.CompilerParams(dimension_semantics=sem,
                                vmem_limit_bytes=VMEM_LIMIT_BYTES)


def _ceil_div(a, b):
    return -(-a // b)


def _rms(x, w):
    return x * lax.rsqrt(jnp.mean(x * x, axis=-1, keepdims=True) + EPS) * w


def _prenorm_kernel(x_ref, w_ref, h_ref):
    h_ref[...] = _rms(x_ref[...].astype(F32), w_ref[...]).astype(h_ref.dtype)


def prenorm(x, w, tr=512):
    M, D = x.shape
    tr = min(tr, M)
    return pl.pallas_call(
        _prenorm_kernel,
        grid=(M // tr,),
        in_specs=[pl.BlockSpec((tr, D), lambda i: (i, 0)),
                  pl.BlockSpec((1, D), lambda i: (0, 0))],
        out_specs=pl.BlockSpec((tr, D), lambda i: (i, 0)),
        out_shape=jax.ShapeDtypeStruct((M, D), BF16),
        compiler_params=_params("parallel"),
        name="prenorm",
    )(x, w.reshape(1, D))


def _post_kernel(y_ref, x_ref, wpost_ref, xo_ref):
    xo_ref[...] = x_ref[...] + _rms(y_ref[...].astype(F32), wpost_ref[...])


def post_norm_residual(y, x, w_post, tr=512):
    M, D = x.shape
    tr = min(tr, M)
    row = pl.BlockSpec((tr, D), lambda i: (i, 0))
    return pl.pallas_call(
        _post_kernel, grid=(M // tr,),
        in_specs=[row, row, pl.BlockSpec((1, D), lambda i: (0, 0))], out_specs=row,
        out_shape=jax.ShapeDtypeStruct((M, D), F32),
        compiler_params=_params("parallel"), name="post_norm",
    )(y, x, w_post.reshape(1, D))


def _weight_spec(w, lead, k_rows, tn, index_map):
    assert w.ndim == len(lead) + 2
    return pl.BlockSpec((None,) * len(lead) + (k_rows, tn),
                        lambda *g: tuple(lead) + index_map(*g))


def _mm_kernel(a_ref, w_ref, o_ref):
    o_ref[...] = jnp.dot(a_ref[...], w_ref[...].astype(BF16),
                         preferred_element_type=F32).astype(o_ref.dtype)


def matmul(a, w, out_dtype, tm, tn, lead=()):
    M, K = a.shape
    N = w.shape[-1]
    assert w.shape[-2] == K and M % tm == 0 and N % tn == 0
    return pl.pallas_call(
        _mm_kernel,
        grid=(N // tn, M // tm),
        in_specs=[pl.BlockSpec((tm, K), lambda j, i: (i, 0)),
                  _weight_spec(w, lead, K, tn, lambda j, i: (0, j))],
        out_specs=pl.BlockSpec((tm, tn), lambda j, i: (i, j)),
        out_shape=jax.ShapeDtypeStruct((M, N), out_dtype),
        compiler_params=_params("parallel", "parallel"),
        name="matmul",
    )(a, w)


def _mm_ksplit_kernel(a_ref, w_ref, o_ref, acc_ref):
    d = jnp.dot(a_ref[...], w_ref[...], preferred_element_type=F32)
    k = pl.program_id(2)
    last = pl.num_programs(2) - 1

    @pl.when(k == 0)
    def _():
        acc_ref[...] = d

    @pl.when((k > 0) & (k < last))
    def _():
        acc_ref[...] += d

    @pl.when(k == last)
    def _():
        o_ref[...] = (acc_ref[...] + d).astype(o_ref.dtype)


def matmul_ksplit(a, w, out_dtype, tm, tn, tk):
    M, K = a.shape
    N = w.shape[1]
    assert M % tm == 0 and N % tn == 0 and K % tk == 0 and K // tk >= 2
    return pl.pallas_call(
        _mm_ksplit_kernel,
        grid=(N // tn, M // tm, K // tk),
        in_specs=[pl.BlockSpec((tm, tk), lambda j, i, k: (i, k)),
                  pl.BlockSpec((tk, tn), lambda j, i, k: (k, j))],
        out_specs=pl.BlockSpec((tm, tn), lambda j, i, k: (i, j)),
        out_shape=jax.ShapeDtypeStruct((M, N), out_dtype),
        scratch_shapes=[pltpu.VMEM((tm, tn), F32)],
        compiler_params=_params("parallel", "parallel", "arbitrary"),
        name="matmul_ksplit",
    )(a, w)


def _side_cast_specs(src, lead, n_steps, step_of):
    assert src.ndim == len(lead) + 2
    R, C = src.shape[-2:]
    pack = 2 * SUBLANES
    rows = _ceil_div(_ceil_div(R, n_steps), pack) * pack
    n_blk = _ceil_div(R, rows)
    idx = lambda *g: (jnp.minimum(step_of(*g), n_blk - 1), 0)
    in_spec = pl.BlockSpec((None,) * len(lead) + (rows, C),
                           lambda *g: tuple(lead) + idx(*g))
    return in_spec, pl.BlockSpec((rows, C), idx), jax.ShapeDtypeStruct((R, C), BF16)


def _in_proj_kernel(a_ref, wt_ref, wgate_ref, src_ref, o_ref, gates_ref, dst_ref):
    nt = (((1,), (1,)), ((), ()))
    a = a_ref[...]
    o_ref[...] = lax.dot_general(a, wt_ref[...].astype(BF16), nt,
                                 preferred_element_type=F32).astype(o_ref.dtype)

    @pl.when(pl.program_id(1) == 0)
    def _():
        gates_ref[...] = lax.dot_general(a, wgate_ref[...].astype(BF16), nt,
                                         preferred_element_type=F32)

    dst_ref[...] = src_ref[...].astype(dst_ref.dtype)


def input_projection(a, wt, row0, gate0, n_gate, n_cols, tm, tn, side_src, side_lead):
    M, K = a.shape
    assert wt.shape[1] == K and M % tm == 0 and n_cols % tn == 0 and gate0 % tn == 0
    assert row0 % SUBLANES == 0 and n_gate % SUBLANES == 0
    assert row0 + n_cols + n_gate <= wt.shape[0]
    n_m, n_n = M // tm, n_cols // tn
    jgate = gate0 // tn

    def w_rows(i, j):
        r = row0 + j * tn + jnp.where(j >= jgate, n_gate, 0)
        return (pl.multiple_of(r, SUBLANES), 0)

    side_in, side_out, side_shape = _side_cast_specs(
        side_src, side_lead, n_m * n_n, lambda i, j: i * n_n + j)
    return pl.pallas_call(
        _in_proj_kernel,
        grid=(n_m, n_n),
        in_specs=[pl.BlockSpec((tm, K), lambda i, j: (i, 0), pipeline_mode=pl.Buffered(1)),
                  pl.BlockSpec((pl.Element(tn), pl.Element(K)), w_rows),
                  pl.BlockSpec((pl.Element(n_gate), pl.Element(K)),
                               lambda i, j: (row0 + gate0, 0)),
                  side_in],
        out_specs=[pl.BlockSpec((tm, tn), lambda i, j: (i, j)),
                   pl.BlockSpec((tm, n_gate), lambda i, j: (i, 0)),
                   side_out],
        out_shape=[jax.ShapeDtypeStruct((M, n_cols), BF16),
                   jax.ShapeDtypeStruct((M, n_gate), F32),
                   side_shape],
        compiler_params=_params("arbitrary", "arbitrary"),
        name="input_projection",
    )(a, wt, wt, side_src)


def _mm_postnorm_kernel(a_ref, w_ref, x_ref, wpost_ref, wpre_ref, xo_ref, h_ref, y_scr,
                        *, n_m):
    i = pl.program_id(0)
    j = pl.program_id(1)
    n_n, _, tn = y_scr.shape[1:]
    rc, d = x_ref.shape

    n_sub = rc // NORM_ROWS
    mm_rows = a_ref.shape[0] // n_sub

    def mm(s):
        rr = slice(s * mm_rows, (s + 1) * mm_rows)
        y_scr[i % 2, j, rr, :] = jnp.dot(a_ref[rr, :], w_ref[...].astype(BF16),
                                         preferred_element_type=F32).astype(y_scr.dtype)

    def norm(s):
        slot = (i + 1) % 2
        cols = [slice(n * tn, (n + 1) * tn) for n in range(n_n)]
        rs = slice(s * NORM_ROWS, (s + 1) * NORM_ROWS)
        rows = pl.ds(pl.multiple_of(j * rc, rc) + s * NORM_ROWS, NORM_ROWS)
        y_tile = lambda n: y_scr[slot, n, rows, :].astype(F32)
        ssq = None
        for n in range(n_n):
            y = y_tile(n)
            ssq = y * y if ssq is None else ssq + y * y
        r_post = lax.rsqrt(jnp.sum(ssq, axis=-1, keepdims=True) / d + EPS)
        ssq = None
        for n, c in enumerate(cols):
            xn = x_ref[rs, c] + y_tile(n) * r_post * wpost_ref[:, c]
            xo_ref[rs, c] = xn
            ssq = xn * xn if ssq is None else ssq + xn * xn
        r_pre = lax.rsqrt(jnp.sum(ssq, axis=-1, keepdims=True) / d + EPS)
        for c in cols:
            h_ref[rs, c] = (xo_ref[rs, c] * r_pre * wpre_ref[:, c]).astype(h_ref.dtype)

    @pl.when(i == 0)
    def _():
        for s in range(n_sub):
            mm(s)

    @pl.when((i > 0) & (i < n_m))
    def _():
        for s in range(n_sub):
            norm(s)
            mm(s)

    @pl.when(i == n_m)
    def _():
        for s in range(n_sub):
            norm(s)


def matmul_post_pre_norm(a, w, x, w_post, w_pre, tm, tn):
    M, K = a.shape
    D = w.shape[1]
    n_m, n_n = M // tm, D // tn
    rc = tm // n_n
    assert M % tm == 0 and D % tn == 0 and tm % n_n == 0
    assert rc % NORM_ROWS == 0 and tm % (rc // NORM_ROWS) == 0
    chunk = pl.BlockSpec(
        (rc, D), lambda i, j: (jnp.maximum(i - 1, 0) * n_n + jnp.where(i == 0, 0, j), 0))
    vec = pl.BlockSpec((1, D), lambda i, j: (0, 0))
    return pl.pallas_call(
        functools.partial(_mm_postnorm_kernel, n_m=n_m),
        grid=(n_m + 1, n_n),
        in_specs=[pl.BlockSpec((tm, K), lambda i, j: (jnp.minimum(i, n_m - 1), 0)),
                  pl.BlockSpec((K, tn), lambda i, j: (0, jnp.where(i == n_m, n_n - 1, j))),
                  chunk, vec, vec],
        out_specs=[chunk, chunk],
        out_shape=[jax.ShapeDtypeStruct((M, D), F32), jax.ShapeDtypeStruct((M, D), BF16)],
        scratch_shapes=[pltpu.VMEM((2, n_n, tm, tn), BF16)],
        compiler_params=_params("arbitrary", "arbitrary"),
        name="matmul_post_pre_norm",
    )(a, w, x, w_post.reshape(1, D), w_pre.reshape(1, D))


def _swiglu_kernel(a_ref, wg_ref, wu_ref, src_ref, o_ref, dst_ref):
    dst_ref[...] = src_ref[...].astype(dst_ref.dtype)
    wg = wg_ref[...].astype(BF16)
    wu = wu_ref[...].astype(BF16)
    for r0 in range(0, o_ref.shape[0], EPILOGUE_ROWS):
        rs = slice(r0, r0 + EPILOGUE_ROWS)
        a = a_ref[rs, :]
        g = jnp.dot(a, wg, preferred_element_type=F32)
        u = jnp.dot(a, wu, preferred_element_type=F32)
        o_ref[rs, :] = (g * jax.nn.sigmoid(g) * u).astype(o_ref.dtype)


def swiglu_up(a, wg, wu, tm, tn, lead, side_src):
    M, K = a.shape
    N = wg.shape[-1]
    assert M % tm == 0 and N % tn == 0
    n_m, n_n = M // tm, N // tn
    wspec = _weight_spec(wg, lead, K, tn, lambda i, j: (0, j))
    side_in, side_out, side_shape = _side_cast_specs(
        side_src, lead, n_m * n_n, lambda i, j: i * n_n + j)
    return pl.pallas_call(
        _swiglu_kernel,
        grid=(n_m, n_n),
        in_specs=[pl.BlockSpec((tm, K), lambda i, j: (i, 0), pipeline_mode=pl.Buffered(1)),
                  wspec, wspec, side_in],
        out_specs=[pl.BlockSpec((tm, tn), lambda i, j: (i, j)), side_out],
        out_shape=[jax.ShapeDtypeStruct((M, N), BF16), side_shape],
        compiler_params=_params("arbitrary", "arbitrary"),
        name="swiglu_up",
    )(a, wg, wu, side_src)


def _branch_kernel(a0_ref, a1_ref, w0_ref, w1_ref, g0_ref, g1_ref, o_ref):
    w0 = w0_ref[...].astype(BF16)
    w1 = w1_ref[...].astype(BF16)
    for r0 in range(0, o_ref.shape[0], EPILOGUE_ROWS):
        rs = slice(r0, r0 + EPILOGUE_ROWS)
        y0 = jnp.dot(a0_ref[rs, :], w0, preferred_element_type=F32)
        y1 = jnp.dot(a1_ref[rs, :], w1, preferred_element_type=F32)
        o_ref[rs, :] = (jax.nn.sigmoid(g0_ref[rs, :].astype(F32)) * y0
                        + jax.nn.sigmoid(g1_ref[rs, :].astype(F32)) * y1).astype(o_ref.dtype)


def branch_merge(a0, a1, w, lead, proj, g_col0, tm, tn):
    M, K = a0.shape
    N = w.shape[-1]
    assert M % tm == 0 and N % tn == 0 and g_col0 % tn == 0
    gb0 = g_col0 // tn
    gb1 = (g_col0 + N) // tn
    aspec = pl.BlockSpec((tm, K), lambda j, i: (i, 0))
    return pl.pallas_call(
        _branch_kernel,
        grid=(N // tn, M // tm),
        in_specs=[aspec, aspec,
                  _weight_spec(w, tuple(lead) + (0,), K, tn, lambda j, i: (0, j)),
                  _weight_spec(w, tuple(lead) + (1,), K, tn, lambda j, i: (0, j)),
                  pl.BlockSpec((tm, tn), lambda j, i: (i, gb0 + j)),
                  pl.BlockSpec((tm, tn), lambda j, i: (i, gb1 + j))],
        out_specs=pl.BlockSpec((tm, tn), lambda j, i: (i, j)),
        out_shape=jax.ShapeDtypeStruct((M, N), BF16),
        compiler_params=_params("parallel", "parallel"),
        name="branch_merge",
    )(a0, a1, w, w, proj, proj)


def _mlstm_kernel(bias_ref, q_ref, k_ref, v_ref, o_ref, gt_ref, hn_ref, out_ref,
                  c_ref, n_ref, m_ref):
    H = MLSTM_HEADS
    L = q_ref.shape[0]
    dqk = q_ref.shape[1] // H
    dv = v_ref.shape[1] // H

    @pl.when(pl.program_id(1) == 0)
    def _():
        c_ref[...] = jnp.zeros_like(c_ref)
        n_ref[...] = jnp.zeros_like(n_ref)
        m_ref[...] = jnp.zeros_like(m_ref)

    pre = GATE_CAP * jnp.tanh((gt_ref[...] + bias_ref[...]) / GATE_CAP)
    lf = jnp.minimum(pre, 0.0) - jnp.log1p(jnp.exp(-jnp.abs(pre)))
    lane = lax.broadcasted_iota(jnp.int32, lf.shape, 1)
    b_all = lf
    sh = 1
    while sh < L:
        b_all = b_all + jnp.where(lane >= sh, pltpu.roll(b_all, sh, axis=1), 0.0)
        sh *= 2

    for h in range(H):
        out, c_new, n_new, m_new = _mlstm_head(
            pre[h:h + 1, :], b_all[H + h:H + h + 1, :],
            q_ref[:, h * dqk:(h + 1) * dqk], k_ref[:, h * dqk:(h + 1) * dqk],
            v_ref[:, h * dv:(h + 1) * dv], o_ref[:, h * dv:(h + 1) * dv],
            hn_ref[:, h * dv:(h + 1) * dv], c_ref[h], n_ref[h], m_ref[h][:, 0:1])
        out_ref[:, h * dv:(h + 1) * dv] = out.astype(out_ref.dtype)
        c_ref[h] = c_new
        n_ref[h] = n_new
        m_ref[h] = jnp.broadcast_to(m_new, m_ref.shape[1:])


def _mlstm_head(li, b_row, q, k, v, o_pre, head_norm, c_old, n_old, m_prev):
    L, dqk = q.shape
    scale = dqk ** -0.5
    b_sq = jnp.broadcast_to(b_row, (L, L)).T
    row = lax.broadcasted_iota(jnp.int32, (L, L), 0)
    col = lax.broadcasted_iota(jnp.int32, (L, L), 1)
    dm = jnp.where(col <= row, b_sq - b_row + li, -jnp.inf)
    inter = b_sq[:, 0:1] + m_prev
    m_row = jnp.maximum(inter, jnp.max(dm, axis=-1, keepdims=True))
    w_intra = jnp.exp(dm - m_row)
    s_inter = jnp.exp(inter - m_row)

    qk = lax.dot_general(q, k, (((1,), (1,)), ((), ())),
                         preferred_element_type=F32) * scale * w_intra
    q_c = jnp.dot(q, c_old.astype(BF16), preferred_element_type=F32) * scale
    num = s_inter * q_c + jnp.dot(qk.astype(BF16), v, preferred_element_type=F32)
    q_n = jnp.sum(q.astype(F32) * n_old, axis=-1, keepdims=True) * scale
    den = s_inter * q_n + jnp.sum(qk, axis=-1, keepdims=True)
    hh = num / jnp.maximum(jnp.abs(den), jnp.exp(-m_row))

    b_last = b_row[:, L - 1:L]
    a_row = b_last - b_row + li
    m_new = jnp.maximum(b_last + m_prev, jnp.max(a_row, axis=-1, keepdims=True))
    s_state = jnp.exp(b_last + m_prev - m_new)
    e_col = jnp.broadcast_to(jnp.exp(a_row - m_new), (L, L)).T[:, 0:1]
    kw = k.astype(F32) * e_col
    c_new = s_state * c_old + lax.dot_general(
        kw.astype(BF16), v, (((0,), (0,)), ((), ())), preferred_element_type=F32)
    n_new = s_state * n_old + jnp.sum(kw, axis=0, keepdims=True)

    out = jax.nn.sigmoid(o_pre.astype(F32)) * _rms(hh, head_norm)
    return out, c_new, n_new, m_new


def mlstm_mixer(proj, gates_t, b_if, head_norm, batch, seq, q_col0, k_col0, v_col0, o_col0,
                dqk, dv):
    L = MLSTM_CHUNK
    T = seq // L
    H = MLSTM_HEADS
    qkw, vw = H * dqk, H * dv
    assert q_col0 % qkw == 0 and k_col0 % qkw == 0 and v_col0 % vw == 0 and o_col0 % vw == 0
    qb, kb, vb, ob = q_col0 // qkw, k_col0 // qkw, v_col0 // vw, o_col0 // vw
    row = lambda cb: (lambda b, t: (b * T + t, cb))
    return pl.pallas_call(
        _mlstm_kernel,
        grid=(batch, T),
        in_specs=[
            pl.BlockSpec((2 * H, 1), lambda b, t: (0, 0)),
            pl.BlockSpec((L, qkw), row(qb)),
            pl.BlockSpec((L, qkw), row(kb)),
            pl.BlockSpec((L, vw), row(vb)),
            pl.BlockSpec((L, vw), row(ob)),
            pl.BlockSpec((2 * H, L), lambda b, t: (0, b * T + t)),
            pl.BlockSpec((1, vw), lambda b, t: (0, 0)),
        ],
        out_specs=pl.BlockSpec((L, vw), row(0)),
        out_shape=jax.ShapeDtypeStruct((batch * seq, vw), BF16),
        scratch_shapes=[pltpu.VMEM((H, dqk, dv), F32),
                        pltpu.VMEM((H, 1, dqk), F32),
                        pltpu.VMEM((H, 1, LANES), F32)],
        compiler_params=_params("parallel", "arbitrary"),
        name="mlstm",
    )(b_if.reshape(2 * H, 1), proj, proj, proj, proj, gates_t, head_norm.reshape(1, vw))


def _conv_kernel(cb_ref, cc_ref, cx_ref, w_ref, o_ref, carry_ref):
    s = pl.program_id(2)
    ts = cb_ref.shape[0]

    @pl.when(s == 0)
    def _():
        carry_ref[...] = jnp.zeros_like(carry_ref)

    w = w_ref[...]
    w0, w1, w2 = w[0:1, :], w[1:2, :], w[2:3, :]
    u = cc_ref[...].astype(F32) * cx_ref[...].astype(F32)
    y = w2 * u + w1 * pltpu.roll(u, 1, axis=0) + w0 * pltpu.roll(u, 2, axis=0)
    o_ref[...] = (cb_ref[...].astype(F32) * y).astype(o_ref.dtype)

    prev = carry_ref[...]
    u_h = u[0:8, :]
    r = lax.broadcasted_iota(jnp.int32, u_h.shape, 0)
    u1 = jnp.where(r < 1, pltpu.roll(prev, 1, axis=0), pltpu.roll(u_h, 1, axis=0))
    u2 = jnp.where(r < 2, pltpu.roll(prev, 2, axis=0), pltpu.roll(u_h, 2, axis=0))
    y_h = w2 * u_h + w1 * u1 + w0 * u2
    o_ref[0:8, :] = (cb_ref[0:8, :].astype(F32) * y_h).astype(o_ref.dtype)
    carry_ref[...] = u[ts - 8:ts, :]


def gated_conv(proj, conv_w, batch, seq, cb_col0, cc_col0, cx_col0, ts=512, tc=512):
    C = conv_w.shape[1]
    nS = seq // ts
    b0, c0, x0 = cb_col0 // tc, cc_col0 // tc, cx_col0 // tc
    return pl.pallas_call(
        _conv_kernel,
        grid=(batch, C // tc, nS),
        in_specs=[
            pl.BlockSpec((ts, tc), lambda b, c, s: (b * nS + s, b0 + c)),
            pl.BlockSpec((ts, tc), lambda b, c, s: (b * nS + s, c0 + c)),
            pl.BlockSpec((ts, tc), lambda b, c, s: (b * nS + s, x0 + c)),
            pl.BlockSpec((CONV_K, tc), lambda b, c, s: (0, c)),
        ],
        out_specs=pl.BlockSpec((ts, tc), lambda b, c, s: (b * nS + s, c)),
        out_shape=jax.ShapeDtypeStruct((batch * seq, C), BF16),
        scratch_shapes=[pltpu.VMEM((8, tc), F32)],
        compiler_params=_params("parallel", "parallel", "arbitrary"),
        name="gated_conv",
    )(proj, proj, proj, conv_w)


def _xq_attn_kernel(a_ref, wq_ref, k_ref, v_ref, src_ref, o_ref, dst_ref, s_scr, *, parts):
    dst_ref[...] = src_ref[...].astype(dst_ref.dtype)
    part = pl.program_id(1) % parts
    dh = v_ref.shape[1]
    q = jnp.dot(a_ref[...], wq_ref[...].astype(BF16),
                preferred_element_type=F32).astype(BF16)
    s_part = lax.dot_general(q, k_ref[...], (((1,), (1,)), ((), ())),
                             preferred_element_type=F32)

    @pl.when(part == 0)
    def _():
        s_scr[...] = s_part

    @pl.when((part > 0) & (part < parts - 1))
    def _():
        s_scr[...] += s_part

    @pl.when(part == parts - 1)
    def _():
        s = (s_scr[...] + s_part) * (dh ** -0.5)
        e = jnp.exp(s - jnp.max(s, axis=-1, keepdims=True))
        p = e / jnp.sum(e, axis=-1, keepdims=True)
        o_ref[...] = jnp.dot(p.astype(BF16), v_ref[...],
                             preferred_element_type=F32).astype(o_ref.dtype)


def xq_cross_attention(a, w_xq, lead, k, v, seq, mem_len, dh, tm, tn, side_src):
    M, K = a.shape
    H = XATTN_HEADS
    parts = dh // tn
    assert M % tm == 0 and seq % tm == 0 and dh % tn == 0 and parts >= 2
    tiles_per_batch = seq // tm
    n_m, n_n = M // tm, H * parts
    side_in, side_out, side_shape = _side_cast_specs(
        side_src, lead, n_m * n_n, lambda i, j: i * n_n + j)
    return pl.pallas_call(
        functools.partial(_xq_attn_kernel, parts=parts),
        grid=(n_m, n_n),
        in_specs=[
            pl.BlockSpec((tm, K), lambda i, j: (i, 0)),
            _weight_spec(w_xq, lead, K, tn, lambda i, j: (0, j)),
            pl.BlockSpec((mem_len, tn), lambda i, j: (i // tiles_per_batch, j)),
            pl.BlockSpec((mem_len, dh), lambda i, j: (i // tiles_per_batch, j // parts)),
            side_in,
        ],
        out_specs=[pl.BlockSpec((tm, dh), lambda i, j: (i, j // parts)), side_out],
        out_shape=[jax.ShapeDtypeStruct((M, H * dh), BF16), side_shape],
        scratch_shapes=[pltpu.VMEM((tm, mem_len), F32)],
        compiler_params=_params("arbitrary", "arbitrary"),
        name="xq_cross_attention",
    )(a, w_xq, k, v, side_src)


def kernel(x, mem, norm_pre_mix, norm_post_mix, w_in, b_if, mlstm_head_norm, conv_w, w_branch, w_mix_out, norm_pre_xattn, norm_post_xattn, norm_mem, w_xq, w_xk, w_xv, w_xo, norm_pre_ffn, norm_post_ffn, w_ffn_gate, w_ffn_up, w_ffn_down):
    B, S, D = x.shape
    mem_len = mem.shape[1]
    depth = w_in.shape[0]
    H = MLSTM_HEADS
    v_width = mlstm_head_norm.shape[1]
    conv_width = conv_w.shape[2]
    dv = v_width // H
    dqk = dv // 2
    qk_width = H * dqk
    dh = w_xq.shape[2] // XATTN_HEADS

    in_width = w_in.shape[2]
    w_in_t = jnp.swapaxes(w_in, 1, 2).reshape(depth * in_width, D)
    q0 = 0
    k0 = q0 + qk_width
    v0 = k0 + qk_width
    o0 = v0 + v_width
    gate0 = o0 + v_width
    cb0 = gate0
    cc0 = cb0 + conv_width
    cx0 = cc0 + conv_width
    g0 = cx0 + conv_width

    xr = x.reshape(B * S, D)
    memr = mem.reshape(B * mem_len, D)
    for l in range(depth):
        h = prenorm(xr, norm_pre_mix[l])
        proj, gates, w_mix_bf = input_projection(
            h, w_in_t, l * in_width, gate0, 2 * H, in_width - 2 * H, 1024, 1024,
            w_mix_out, (l,))
        h_a = mlstm_mixer(proj, gates.T, b_if[l], mlstm_head_norm[l], B, S,
                          q0, k0, v0, o0, dqk, dv)
        h_b = gated_conv(proj, conv_w[l], B, S, cb0, cc0, cx0, ts=1024, tc=1024)
        merged = branch_merge(h_a, h_b, w_branch, (l,), proj, g0, 1024, 512)
        xr, h = matmul_post_pre_norm(merged, w_mix_bf, xr,
                                     norm_post_mix[l], norm_pre_xattn[l], 1024, 512)

        m = prenorm(memr, norm_mem[l])
        k = matmul(m, w_xk, BF16, B * mem_len, 512, lead=(l,))
        v = matmul(m, w_xv, BF16, B * mem_len, 512, lead=(l,))
        o, w_xo_bf = xq_cross_attention(h, w_xq, (l,), k, v, S, mem_len, dh, 1024, 512,
                                        w_xo)
        xr, h = matmul_post_pre_norm(o, w_xo_bf, xr,
                                     norm_post_xattn[l], norm_pre_ffn[l], 1024, 512)

        hidden, w_down_bf = swiglu_up(h, w_ffn_gate, w_ffn_up, 2048, 256, (l,), w_ffn_down)
        y = matmul_ksplit(hidden, w_down_bf, BF16, 1024, 1024, w_down_bf.shape[0] // 2)
        xr = post_norm_residual(y, xr, norm_post_ffn[l])
    return xr.reshape(B, S, D)
```

```python
import functools

import jax
import jax.numpy as jnp
from jax import lax
from jax.experimental import pallas as pl
from jax.experimental.pallas import tpu as pltpu

F32 = jnp.float32
BF16 = jnp.bfloat16

MLSTM_HEADS = 4
XATTN_HEADS = 4
CONV_K = 3
GATE_CAP = 15.0
EPS = 1e-6
MLSTM_CHUNK = 256
LANES = 128
SUBLANES = 8
NORM_ROWS = 32
EPILOGUE_ROWS = 256

VMEM_LIMIT_BYTES = 56 * 1024 * 1024


def _params(*sem):
    return pltpu.CompilerParams(dimension_semantics=sem,
                                vmem_limit_bytes=VMEM_LIMIT_BYTES)


def _ceil_div(a, b):
    return -(-a // b)


def _rms(x, w):
    return x * lax.rsqrt(jnp.mean(x * x, axis=-1, keepdims=True) + EPS) * w


def _prenorm_kernel(x_ref, w_ref, h_ref):
    h_ref[...] = _rms(x_ref[...].astype(F32), w_ref[...]).astype(h_ref.dtype)


def prenorm(x, w, tr=512):
    M, D = x.shape
    tr = min(tr, M)
    return pl.pallas_call(
        _prenorm_kernel,
        grid=(M // tr,),
        in_specs=[pl.BlockSpec((tr, D), lambda i: (i, 0)),
                  pl.BlockSpec((1, D), lambda i: (0, 0))],
        out_specs=pl.BlockSpec((tr, D), lambda i: (i, 0)),
        out_shape=jax.ShapeDtypeStruct((M, D), BF16),
        compiler_params=_params("parallel"),
        name="prenorm",
    )(x, w.reshape(1, D))


def _post_kernel(y_ref, x_ref, wpost_ref, xo_ref):
    xo_ref[...] = x_ref[...] + _rms(y_ref[...].astype(F32), wpost_ref[...])


def post_norm_residual(y, x, w_post, tr=512):
    M, D = x.shape
    tr = min(tr, M)
    row = pl.BlockSpec((tr, D), lambda i: (i, 0))
    return pl.pallas_call(
        _post_kernel, grid=(M // tr,),
        in_specs=[row, row, pl.BlockSpec((1, D), lambda i: (0, 0))], out_specs=row,
        out_shape=jax.ShapeDtypeStruct((M, D), F32),
        compiler_params=_params("parallel"), name="post_norm",
    )(y, x, w_post.reshape(1, D))


def _weight_spec(w, lead, k_rows, tn, index_map):
    assert w.ndim == len(lead) + 2
    return pl.BlockSpec((None,) * len(lead) + (k_rows, tn),
                        lambda *g: tuple(lead) + index_map(*g))


def _mm_kernel(a_ref, w_ref, o_ref):
    o_ref[...] = jnp.dot(a_ref[...], w_ref[...].astype(BF16),
                         preferred_element_type=F32).astype(o_ref.dtype)


def matmul(a, w, out_dtype, tm, tn, lead=()):
    M, K = a.shape
    N = w.shape[-1]
    assert w.shape[-2] == K and M % tm == 0 and N % tn == 0
    return pl.pallas_call(
        _mm_kernel,
        grid=(N // tn, M // tm),
        in_specs=[pl.BlockSpec((tm, K), lambda j, i: (i, 0)),
                  _weight_spec(w, lead, K, tn, lambda j, i: (0, j))],
        out_specs=pl.BlockSpec((tm, tn), lambda j, i: (i, j)),
        out_shape=jax.ShapeDtypeStruct((M, N), out_dtype),
        compiler_params=_params("parallel", "parallel"),
        name="matmul",
    )(a, w)


def _mm_ksplit_kernel(a_ref, w_ref, o_ref, acc_ref):
    d = jnp.dot(a_ref[...], w_ref[...], preferred_element_type=F32)
    k = pl.program_id(2)
    last = pl.num_programs(2) - 1

    @pl.when(k == 0)
    def _():
        acc_ref[...] = d

    @pl.when((k > 0) & (k < last))
    def _():
        acc_ref[...] += d

    @pl.when(k == last)
    def _():
        o_ref[...] = (acc_ref[...] + d).astype(o_ref.dtype)


def matmul_ksplit(a, w, out_dtype, tm, tn, tk):
    M, K = a.shape
    N = w.shape[1]
    assert M % tm == 0 and N % tn == 0 and K % tk == 0 and K // tk >= 2
    return pl.pallas_call(
        _mm_ksplit_kernel,
        grid=(N // tn, M // tm, K // tk),
        in_specs=[pl.BlockSpec((tm, tk), lambda j, i, k: (i, k)),
                  pl.BlockSpec((tk, tn), lambda j, i, k: (k, j))],
        out_specs=pl.BlockSpec((tm, tn), lambda j, i, k: (i, j)),
        out_shape=jax.ShapeDtypeStruct((M, N), out_dtype),
        scratch_shapes=[pltpu.VMEM((tm, tn), F32)],
        compiler_params=_params("parallel", "parallel", "arbitrary"),
        name="matmul_ksplit",
    )(a, w)


def _side_cast_specs(src, lead, n_steps, step_of):
    assert src.ndim == len(lead) + 2
    R, C = src.shape[-2:]
    pack = 2 * SUBLANES
    rows = _ceil_div(_ceil_div(R, n_steps), pack) * pack
    n_blk = _ceil_div(R, rows)
    idx = lambda *g: (jnp.minimum(step_of(*g), n_blk - 1), 0)
    in_spec = pl.BlockSpec((None,) * len(lead) + (rows, C),
                           lambda *g: tuple(lead) + idx(*g))
    return in_spec, pl.BlockSpec((rows, C), idx), jax.ShapeDtypeStruct((R, C), BF16)


def _side_casts(sides, n_steps, step_of):
    specs = [_side_cast_specs(src, lead, n_steps, step_of) for src, lead in sides]
    return ([s[0] for s in specs], [s[1] for s in specs], [s[2] for s in specs],
            [src for src, _ in sides])


def _round_sides(src_refs, dst_refs):
    for src_ref, dst_ref in zip(src_refs, dst_refs):
        dst_ref[...] = src_ref[...].astype(dst_ref.dtype)


def _in_proj_kernel(a_ref, wt_ref, wgate_ref, src_ref, o_ref, gates_ref, dst_ref):
    nt = (((1,), (1,)), ((), ()))
    a = a_ref[...]
    o_ref[...] = lax.dot_general(a, wt_ref[...].astype(BF16), nt,
                                 preferred_element_type=F32).astype(o_ref.dtype)

    @pl.when(pl.program_id(1) == 0)
    def _():
        gates_ref[...] = lax.dot_general(a, wgate_ref[...].astype(BF16), nt,
                                         preferred_element_type=F32)

    dst_ref[...] = src_ref[...].astype(dst_ref.dtype)


def input_projection(a, wt, row0, gate0, n_gate, n_cols, tm, tn, side_src, side_lead):
    M, K = a.shape
    assert wt.shape[1] == K and M % tm == 0 and n_cols % tn == 0 and gate0 % tn == 0
    assert row0 % SUBLANES == 0 and n_gate % SUBLANES == 0
    assert row0 + n_cols + n_gate <= wt.shape[0]
    n_m, n_n = M // tm, n_cols // tn
    jgate = gate0 // tn

    def w_rows(i, j):
        r = row0 + j * tn + jnp.where(j >= jgate, n_gate, 0)
        return (pl.multiple_of(r, SUBLANES), 0)

    side_in, side_out, side_shape = _side_cast_specs(
        side_src, side_lead, n_m * n_n, lambda i, j: i * n_n + j)
    return pl.pallas_call(
        _in_proj_kernel,
        grid=(n_m, n_n),
        in_specs=[pl.BlockSpec((tm, K), lambda i, j: (i, 0), pipeline_mode=pl.Buffered(1)),
                  pl.BlockSpec((pl.Element(tn), pl.Element(K)), w_rows),
                  pl.BlockSpec((pl.Element(n_gate), pl.Element(K)),
                               lambda i, j: (row0 + gate0, 0)),
                  side_in],
        out_specs=[pl.BlockSpec((tm, tn), lambda i, j: (i, j)),
                   pl.BlockSpec((tm, n_gate), lambda i, j: (i, 0)),
                   side_out],
        out_shape=[jax.ShapeDtypeStruct((M, n_cols), BF16),
                   jax.ShapeDtypeStruct((M, n_gate), F32),
                   side_shape],
        compiler_params=_params("arbitrary", "arbitrary"),
        name="input_projection",
    )(a, wt, wt, side_src)


def _mm_postnorm_kernel(a_ref, w_ref, x_ref, wpost_ref, wpre_ref, xo_ref, h_ref, y_scr,
                        *, n_m):
    i = pl.program_id(0)
    j = pl.program_id(1)
    n_n, _, tn = y_scr.shape[1:]
    rc, d = x_ref.shape

    n_sub = rc // NORM_ROWS
    mm_rows = a_ref.shape[0] // n_sub

    def mm(s):
        rr = slice(s * mm_rows, (s + 1) * mm_rows)
        y_scr[i % 2, j, rr, :] = jnp.dot(a_ref[rr, :], w_ref[...].astype(BF16),
                                         preferred_element_type=F32).astype(y_scr.dtype)

    def norm(s):
        slot = (i + 1) % 2
        cols = [slice(n * tn, (n + 1) * tn) for n in range(n_n)]
        rs = slice(s * NORM_ROWS, (s + 1) * NORM_ROWS)
        rows = pl.ds(pl.multiple_of(j * rc, rc) + s * NORM_ROWS, NORM_ROWS)
        y_tile = lambda n: y_scr[slot, n, rows, :].astype(F32)
        ssq = None
        for n in range(n_n):
            y = y_tile(n)
            ssq = y * y if ssq is None else ssq + y * y
        r_post = lax.rsqrt(jnp.sum(ssq, axis=-1, keepdims=True) / d + EPS)
        ssq = None
        for n, c in enumerate(cols):
            xn = x_ref[rs, c] + y_tile(n) * r_post * wpost_ref[:, c]
            xo_ref[rs, c] = xn
            ssq = xn * xn if ssq is None else ssq + xn * xn
        r_pre = lax.rsqrt(jnp.sum(ssq, axis=-1, keepdims=True) / d + EPS)
        for c in cols:
            h_ref[rs, c] = (xo_ref[rs, c] * r_pre * wpre_ref[:, c]).astype(h_ref.dtype)

    @pl.when(i == 0)
    def _():
        for s in range(n_sub):
            mm(s)

    @pl.when((i > 0) & (i < n_m))
    def _():
        for s in range(n_sub):
            norm(s)
            mm(s)

    @pl.when(i == n_m)
    def _():
        for s in range(n_sub):
            norm(s)


def matmul_post_pre_norm(a, w, x, w_post, w_pre, tm, tn):
    M, K = a.shape
    D = w.shape[1]
    n_m, n_n = M // tm, D // tn
    rc = tm // n_n
    assert M % tm == 0 and D % tn == 0 and tm % n_n == 0
    assert rc % NORM_ROWS == 0 and tm % (rc // NORM_ROWS) == 0
    chunk = pl.BlockSpec(
        (rc, D), lambda i, j: (jnp.maximum(i - 1, 0) * n_n + jnp.where(i == 0, 0, j), 0))
    vec = pl.BlockSpec((1, D), lambda i, j: (0, 0))
    return pl.pallas_call(
        functools.partial(_mm_postnorm_kernel, n_m=n_m),
        grid=(n_m + 1, n_n),
        in_specs=[pl.BlockSpec((tm, K), lambda i, j: (jnp.minimum(i, n_m - 1), 0)),
                  pl.BlockSpec((K, tn), lambda i, j: (0, jnp.where(i == n_m, n_n - 1, j))),
                  chunk, vec, vec],
        out_specs=[chunk, chunk],
        out_shape=[jax.ShapeDtypeStruct((M, D), F32), jax.ShapeDtypeStruct((M, D), BF16)],
        scratch_shapes=[pltpu.VMEM((2, n_n, tm, tn), BF16)],
        compiler_params=_params("arbitrary", "arbitrary"),
        name="matmul_post_pre_norm",
    )(a, w, x, w_post.reshape(1, D), w_pre.reshape(1, D))


def _swiglu_kernel(a_ref, wg_ref, wu_ref, src_ref, o_ref, dst_ref):
    dst_ref[...] = src_ref[...].astype(dst_ref.dtype)
    wg = wg_ref[...]
    wu = wu_ref[...]
    for r0 in range(0, o_ref.shape[0], EPILOGUE_ROWS):
        rs = slice(r0, r0 + EPILOGUE_ROWS)
        a = a_ref[rs, :]
        g = jnp.dot(a, wg, preferred_element_type=F32)
        u = jnp.dot(a, wu, preferred_element_type=F32)
        o_ref[rs, :] = (g * jax.nn.sigmoid(g) * u).astype(o_ref.dtype)


def swiglu_up(a, wg, wu, tm, tn, side_src, side_lead):
    M, K = a.shape
    N = wg.shape[-1]
    assert M % tm == 0 and N % tn == 0
    n_m, n_n = M // tm, N // tn
    wspec = _weight_spec(wg, (), K, tn, lambda i, j: (0, j))
    side_in, side_out, side_shape = _side_cast_specs(
        side_src, side_lead, n_m * n_n, lambda i, j: i * n_n + j)
    return pl.pallas_call(
        _swiglu_kernel,
        grid=(n_m, n_n),
        in_specs=[pl.BlockSpec((tm, K), lambda i, j: (i, 0), pipeline_mode=pl.Buffered(1)),
                  wspec, wspec, side_in],
        out_specs=[pl.BlockSpec((tm, tn), lambda i, j: (i, j)), side_out],
        out_shape=[jax.ShapeDtypeStruct((M, N), BF16), side_shape],
        compiler_params=_params("arbitrary", "arbitrary"),
        name="swiglu_up",
    )(a, wg, wu, side_src)


def _branch_kernel(a0_ref, a1_ref, w0_ref, w1_ref, g0_ref, g1_ref, *rest):
    n_side = len(rest) // 2
    o_ref = rest[n_side]
    _round_sides(rest[:n_side], rest[n_side + 1:])
    w0 = w0_ref[...].astype(BF16)
    w1 = w1_ref[...].astype(BF16)
    for r0 in range(0, o_ref.shape[0], EPILOGUE_ROWS):
        rs = slice(r0, r0 + EPILOGUE_ROWS)
        y0 = jnp.dot(a0_ref[rs, :], w0, preferred_element_type=F32)
        y1 = jnp.dot(a1_ref[rs, :], w1, preferred_element_type=F32)
        o_ref[rs, :] = (jax.nn.sigmoid(g0_ref[rs, :].astype(F32)) * y0
                        + jax.nn.sigmoid(g1_ref[rs, :].astype(F32)) * y1).astype(o_ref.dtype)


def branch_merge(a0, a1, w, lead, proj, g_col0, tm, tn, sides):
    M, K = a0.shape
    N = w.shape[-1]
    assert M % tm == 0 and N % tn == 0 and g_col0 % tn == 0
    gb0 = g_col0 // tn
    gb1 = (g_col0 + N) // tn
    n_n, n_m = N // tn, M // tm
    aspec = pl.BlockSpec((tm, K), lambda j, i: (i, 0))
    side_in, side_out, side_shapes, side_srcs = _side_casts(
        sides, n_n * n_m, lambda j, i: j * n_m + i)
    return pl.pallas_call(
        _branch_kernel,
        grid=(n_n, n_m),
        in_specs=[aspec, aspec,
                  _weight_spec(w, tuple(lead) + (0,), K, tn, lambda j, i: (0, j)),
                  _weight_spec(w, tuple(lead) + (1,), K, tn, lambda j, i: (0, j)),
                  pl.BlockSpec((tm, tn), lambda j, i: (i, gb0 + j)),
                  pl.BlockSpec((tm, tn), lambda j, i: (i, gb1 + j))] + side_in,
        out_specs=[pl.BlockSpec((tm, tn), lambda j, i: (i, j))] + side_out,
        out_shape=[jax.ShapeDtypeStruct((M, N), BF16)] + side_shapes,
        compiler_params=_params("arbitrary", "arbitrary"),
        name="branch_merge",
    )(a0, a1, w, w, proj, proj, *side_srcs)


def _mlstm_kernel(bias_ref, q_ref, k_ref, v_ref, o_ref, gt_ref, hn_ref, *rest):
    n_side = (len(rest) - 4) // 2
    out_ref = rest[n_side]
    c_ref, n_ref, m_ref = rest[2 * n_side + 1:]
    _round_sides(rest[:n_side], rest[n_side + 1:2 * n_side + 1])
    H = MLSTM_HEADS
    nb, L = q_ref.shape[:2]
    dqk = q_ref.shape[2] // H
    dv = v_ref.shape[2] // H
    scale = dqk ** -0.5
    streams = range(nb * H)
    qs = lambda s: (s // H, slice(None), slice((s % H) * dqk, (s % H + 1) * dqk))
    vs = lambda s: (s // H, slice(None), slice((s % H) * dv, (s % H + 1) * dv))
    hv = lambda s: (slice(None), slice((s % H) * dv, (s % H + 1) * dv))

    @pl.when(pl.program_id(1) == 0)
    def _():
        c_ref[...] = jnp.zeros_like(c_ref)
        n_ref[...] = jnp.zeros_like(n_ref)
        m_ref[...] = jnp.zeros_like(m_ref)

    li, b_row = [], []
    for b in range(nb):
        pre = GATE_CAP * jnp.tanh((gt_ref[b] + bias_ref[...]) / GATE_CAP)
        lf = jnp.minimum(pre, 0.0) - jnp.log1p(jnp.exp(-jnp.abs(pre)))
        lane = lax.broadcasted_iota(jnp.int32, lf.shape, 1)
        b_all = lf
        sh = 1
        while sh < L:
            b_all = b_all + jnp.where(lane >= sh, pltpu.roll(b_all, sh, axis=1), 0.0)
            sh *= 2
        li += [pre[h:h + 1, :] for h in range(H)]
        b_row += [b_all[H + h:H + h + 1, :] for h in range(H)]
    m_prev = [m_ref[s][:, 0:1] for s in streams]

    row = lax.broadcasted_iota(jnp.int32, (L, L), 0)
    col = lax.broadcasted_iota(jnp.int32, (L, L), 1)
    b_sq = [jnp.broadcast_to(b_row[s], (L, L)).T for s in streams]
    dm = [jnp.where(col <= row, b_sq[s] - b_row[s] + li[s], -jnp.inf) for s in streams]
    inter = [b_sq[s][:, 0:1] + m_prev[s] for s in streams]
    m_row = [jnp.maximum(inter[s], jnp.max(dm[s], axis=-1, keepdims=True)) for s in streams]
    w_intra = [jnp.exp(dm[s] - m_row[s]) for s in streams]
    s_inter = [jnp.exp(inter[s] - m_row[s]) for s in streams]

    b_last = [b_row[s][:, L - 1:L] for s in streams]
    a_row = [b_last[s] - b_row[s] + li[s] for s in streams]
    m_new = [jnp.maximum(b_last[s] + m_prev[s], jnp.max(a_row[s], axis=-1, keepdims=True))
             for s in streams]
    s_state = [jnp.exp(b_last[s] + m_prev[s] - m_new[s]) for s in streams]
    e_col = [jnp.broadcast_to(jnp.exp(a_row[s] - m_new[s]), (L, L)).T[:, 0:1]
             for s in streams]

    nt = (((1,), (1,)), ((), ()))
    c_old = [c_ref[s] for s in streams]
    n_old = [n_ref[s] for s in streams]
    qk = [lax.dot_general(q_ref[qs(s)], k_ref[qs(s)], nt, preferred_element_type=F32)
          * scale * w_intra[s] for s in streams]
    q_c = [jnp.dot(q_ref[qs(s)], c_old[s].astype(BF16), preferred_element_type=F32) * scale
           for s in streams]
    num = [s_inter[s] * q_c[s] + jnp.dot(qk[s].astype(BF16), v_ref[vs(s)],
                                         preferred_element_type=F32) for s in streams]
    q_n = [jnp.sum(q_ref[qs(s)].astype(F32) * n_old[s], axis=-1, keepdims=True) * scale
           for s in streams]
    den = [s_inter[s] * q_n[s] + jnp.sum(qk[s], axis=-1, keepdims=True) for s in streams]
    for s in streams:
        hh = num[s] / jnp.maximum(jnp.abs(den[s]), jnp.exp(-m_row[s]))
        out_ref[vs(s)] = (jax.nn.sigmoid(o_ref[vs(s)].astype(F32))
                          * _rms(hh, hn_ref[hv(s)])).astype(out_ref.dtype)

    for s in streams:
        kw = k_ref[qs(s)].astype(F32) * e_col[s]
        c_ref[s] = s_state[s] * c_old[s] + lax.dot_general(
            kw.astype(BF16), v_ref[vs(s)], (((0,), (0,)), ((), ())),
            preferred_element_type=F32)
        n_ref[s] = s_state[s] * n_old[s] + jnp.sum(kw, axis=0, keepdims=True)
        m_ref[s] = jnp.broadcast_to(m_new[s], m_ref.shape[1:])


def mlstm_mixer(proj, gates, b_if, head_norm, batch, seq, q_col0, k_col0, v_col0, o_col0,
                dqk, dv, sides):
    L = MLSTM_CHUNK
    T = seq // L
    H = MLSTM_HEADS
    nb = 1
    qkw, vw = H * dqk, H * dv
    assert q_col0 % qkw == 0 and k_col0 % qkw == 0 and v_col0 % vw == 0 and o_col0 % vw == 0
    qb, kb, vb, ob = q_col0 // qkw, k_col0 // qkw, v_col0 // vw, o_col0 // vw
    proj3 = proj.reshape(batch, seq, proj.shape[1])
    gates_t = jnp.swapaxes(gates.reshape(batch, seq, 2 * H), 1, 2)
    rows = lambda cb: (lambda g, t: (g, t, cb))
    n_g = batch // nb
    side_in, side_out, side_shapes, side_srcs = _side_casts(
        sides, n_g * T, lambda g, t: g * T + t)
    out, *rounded = pl.pallas_call(
        _mlstm_kernel,
        grid=(n_g, T),
        in_specs=[
            pl.BlockSpec((2 * H, 1), lambda g, t: (0, 0)),
            pl.BlockSpec((nb, L, qkw), rows(qb)),
            pl.BlockSpec((nb, L, qkw), rows(kb)),
            pl.BlockSpec((nb, L, vw), rows(vb)),
            pl.BlockSpec((nb, L, vw), rows(ob)),
            pl.BlockSpec((nb, 2 * H, L), lambda g, t: (g, 0, t)),
            pl.BlockSpec((1, vw), lambda g, t: (0, 0)),
        ] + side_in,
        out_specs=[pl.BlockSpec((nb, L, vw), rows(0))] + side_out,
        out_shape=[jax.ShapeDtypeStruct((batch, seq, vw), BF16)] + side_shapes,
        scratch_shapes=[pltpu.VMEM((nb * H, dqk, dv), F32),
                        pltpu.VMEM((nb * H, 1, dqk), F32),
                        pltpu.VMEM((nb * H, 1, LANES), F32)],
        compiler_params=_params("arbitrary", "arbitrary"),
        name="mlstm",
    )(b_if.reshape(2 * H, 1), proj3, proj3, proj3, proj3, gates_t, head_norm.reshape(1, vw),
      *side_srcs)
    return (out.reshape(batch * seq, vw), *rounded)


def _conv_kernel(cb_ref, cc_ref, cx_ref, w_ref, o_ref, carry_ref):
    s = pl.program_id(2)
    ts = cb_ref.shape[0]

    @pl.when(s == 0)
    def _():
        carry_ref[...] = jnp.zeros_like(carry_ref)

    w = w_ref[...]
    w0, w1, w2 = w[0:1, :], w[1:2, :], w[2:3, :]
    u = cc_ref[...].astype(F32) * cx_ref[...].astype(F32)
    y = w2 * u + w1 * pltpu.roll(u, 1, axis=0) + w0 * pltpu.roll(u, 2, axis=0)
    o_ref[...] = (cb_ref[...].astype(F32) * y).astype(o_ref.dtype)

    prev = carry_ref[...]
    u_h = u[0:8, :]
    r = lax.broadcasted_iota(jnp.int32, u_h.shape, 0)
    u1 = jnp.where(r < 1, pltpu.roll(prev, 1, axis=0), pltpu.roll(u_h, 1, axis=0))
    u2 = jnp.where(r < 2, pltpu.roll(prev, 2, axis=0), pltpu.roll(u_h, 2, axis=0))
    y_h = w2 * u_h + w1 * u1 + w0 * u2
    o_ref[0:8, :] = (cb_ref[0:8, :].astype(F32) * y_h).astype(o_ref.dtype)
    carry_ref[...] = u[ts - 8:ts, :]


def gated_conv(proj, conv_w, batch, seq, cb_col0, cc_col0, cx_col0, ts=512, tc=512):
    C = conv_w.shape[1]
    nS = seq // ts
    b0, c0, x0 = cb_col0 // tc, cc_col0 // tc, cx_col0 // tc
    return pl.pallas_call(
        _conv_kernel,
        grid=(batch, C // tc, nS),
        in_specs=[
            pl.BlockSpec((ts, tc), lambda b, c, s: (b * nS + s, b0 + c)),
            pl.BlockSpec((ts, tc), lambda b, c, s: (b * nS + s, c0 + c)),
            pl.BlockSpec((ts, tc), lambda b, c, s: (b * nS + s, x0 + c)),
            pl.BlockSpec((CONV_K, tc), lambda b, c, s: (0, c)),
        ],
        out_specs=pl.BlockSpec((ts, tc), lambda b, c, s: (b * nS + s, c)),
        out_shape=jax.ShapeDtypeStruct((batch * seq, C), BF16),
        scratch_shapes=[pltpu.VMEM((8, tc), F32)],
        compiler_params=_params("parallel", "parallel", "arbitrary"),
        name="gated_conv",
    )(proj, proj, proj, conv_w)


def _xq_attn_kernel(a_ref, wq_ref, k_ref, v_ref, src_ref, o_ref, dst_ref):
    dst_ref[...] = src_ref[...].astype(dst_ref.dtype)
    dh = v_ref.shape[1]
    q = jnp.dot(a_ref[...], wq_ref[...], preferred_element_type=F32).astype(BF16)
    s = lax.dot_general(q, k_ref[...], (((1,), (1,)), ((), ())),
                        preferred_element_type=F32) * (dh ** -0.5)
    e = jnp.exp(s - jnp.max(s, axis=-1, keepdims=True))
    p = e / jnp.sum(e, axis=-1, keepdims=True)
    o_ref[...] = jnp.dot(p.astype(BF16), v_ref[...],
                         preferred_element_type=F32).astype(o_ref.dtype)


def xq_cross_attention(a, w_xq, k, v, seq, mem_len, dh, tm, side_src, side_lead):
    M, K = a.shape
    H = XATTN_HEADS
    assert M % tm == 0 and seq % tm == 0 and w_xq.shape == (K, H * dh)
    tiles_per_batch = seq // tm
    n_m = M // tm
    side_in, side_out, side_shape = _side_cast_specs(
        side_src, side_lead, n_m * H, lambda i, j: i * H + j)
    kv_spec = pl.BlockSpec((mem_len, dh), lambda i, j: (i // tiles_per_batch, j))
    return pl.pallas_call(
        _xq_attn_kernel,
        grid=(n_m, H),
        in_specs=[pl.BlockSpec((tm, K), lambda i, j: (i, 0)),
                  pl.BlockSpec((K, dh), lambda i, j: (0, j)),
                  kv_spec, kv_spec, side_in],
        out_specs=[pl.BlockSpec((tm, dh), lambda i, j: (i, j)), side_out],
        out_shape=[jax.ShapeDtypeStruct((M, H * dh), BF16), side_shape],
        compiler_params=_params("arbitrary", "arbitrary"),
        name="xq_cross_attention",
    )(a, w_xq, k, v, side_src)


def kernel(x, mem, norm_pre_mix, norm_post_mix, w_in, b_if, mlstm_head_norm, conv_w, w_branch, w_mix_out, norm_pre_xattn, norm_post_xattn, norm_mem, w_xq, w_xk, w_xv, w_xo, norm_pre_ffn, norm_post_ffn, w_ffn_gate, w_ffn_up, w_ffn_down):
    B, S, D = x.shape
    mem_len = mem.shape[1]
    depth = w_in.shape[0]
    H = MLSTM_HEADS
    v_width = mlstm_head_norm.shape[1]
    conv_width = conv_w.shape[2]
    dv = v_width // H
    dqk = dv // 2
    qk_width = H * dqk
    dh = w_xq.shape[2] // XATTN_HEADS

    in_width = w_in.shape[2]
    w_in_t = jnp.swapaxes(w_in, 1, 2).reshape(depth * in_width, D)
    q0 = 0
    k0 = q0 + qk_width
    v0 = k0 + qk_width
    o0 = v0 + v_width
    gate0 = o0 + v_width
    cb0 = gate0
    cc0 = cb0 + conv_width
    cx0 = cc0 + conv_width
    g0 = cx0 + conv_width

    xr = x.reshape(B * S, D)
    memr = mem.reshape(B * mem_len, D)
    for l in range(depth):
        h = prenorm(xr, norm_pre_mix[l])
        proj, gates, w_mix_bf = input_projection(
            h, w_in_t, l * in_width, gate0, 2 * H, in_width - 2 * H, 1024, 1024,
            w_mix_out, (l,))
        h_a, w_up_bf = mlstm_mixer(proj, gates, b_if[l], mlstm_head_norm[l], B, S,
                                   q0, k0, v0, o0, dqk, dv, [(w_ffn_up, (l,))])
        h_b = gated_conv(proj, conv_w[l], B, S, cb0, cc0, cx0, ts=1024, tc=1024)
        merged, w_gate_bf, w_xq_bf = branch_merge(
            h_a, h_b, w_branch, (l,), proj, g0, 1024, 512,
            [(w_ffn_gate, (l,)), (w_xq, (l,))])
        xr, h = matmul_post_pre_norm(merged, w_mix_bf, xr,
                                     norm_post_mix[l], norm_pre_xattn[l], 1024, 512)

        m = prenorm(memr, norm_mem[l])
        k = matmul(m, w_xk, BF16, B * mem_len, 512, lead=(l,))
        v = matmul(m, w_xv, BF16, B * mem_len, 512, lead=(l,))
        o, w_xo_bf = xq_cross_attention(h, w_xq_bf, k, v, S, mem_len, dh, 1024, w_xo, (l,))
        xr, h = matmul_post_pre_norm(o, w_xo_bf, xr,
                                     norm_post_xattn[l], norm_pre_ffn[l], 1024, 512)

        hidden, w_down_bf = swiglu_up(h, w_gate_bf, w_up_bf, 4096, 256, w_ffn_down, (l,))
        y = matmul_ksplit(hidden, w_down_bf, BF16, 1024, 1024, w_down_bf.shape[0] // 2)
        xr = post_norm_residual(y, xr, norm_post_ffn[l])
    return xr.reshape(B, S, D)
```

```python
import functools

import jax
import jax.numpy as jnp
from jax import lax
from jax.experimental import pallas as pl
from jax.experimental.pallas import tpu as pltpu

F32 = jnp.float32
BF16 = jnp.bfloat16

MLSTM_HEADS = 4
XATTN_HEADS = 4
CONV_K = 3
GATE_CAP = 15.0
EPS = 1e-6
MLSTM_CHUNK = 256
LANES = 128
SUBLANES = 8
NORM_ROWS = 32
EPILOGUE_ROWS = 256

VMEM_LIMIT_BYTES = 56 * 1024 * 1024
VMEM_LIMIT_LARGE_BYTES = 60 * 1024 * 1024


def _params(*sem):
    return pltpu.CompilerParams(dimension_semantics=sem,
                                vmem_limit_bytes=VMEM_LIMIT_BYTES)


def _ceil_div(a, b):
    return -(-a // b)


def _rms(x, w):
    return x * lax.rsqrt(jnp.mean(x * x, axis=-1, keepdims=True) + EPS) * w


def _prenorm_kernel(x_ref, w_ref, h_ref):
    h_ref[...] = _rms(x_ref[...].astype(F32), w_ref[...]).astype(h_ref.dtype)


def prenorm(x, w, tr=512):
    M, D = x.shape
    tr = min(tr, M)
    return pl.pallas_call(
        _prenorm_kernel,
        grid=(M // tr,),
        in_specs=[pl.BlockSpec((tr, D), lambda i: (i, 0)),
                  pl.BlockSpec((1, D), lambda i: (0, 0))],
        out_specs=pl.BlockSpec((tr, D), lambda i: (i, 0)),
        out_shape=jax.ShapeDtypeStruct((M, D), BF16),
        compiler_params=_params("parallel"),
        name="prenorm",
    )(x, w.reshape(1, D))


def _prenorm_gates_kernel(x_ref, w_ref, wgate_ref, h_ref, gates_ref):
    h = _rms(x_ref[...].astype(F32), w_ref[...]).astype(h_ref.dtype)
    h_ref[...] = h
    gates_ref[...] = lax.dot_general(h, wgate_ref[...].astype(BF16), (((1,), (1,)), ((), ())),
                                     preferred_element_type=F32)


def prenorm_gates(x, w, wt, gate_row0, n_gate, tr=512):
    M, D = x.shape
    assert M % tr == 0 and wt.shape[1] == D and gate_row0 % SUBLANES == 0
    return pl.pallas_call(
        _prenorm_gates_kernel,
        grid=(M // tr,),
        in_specs=[pl.BlockSpec((tr, D), lambda i: (i, 0)),
                  pl.BlockSpec((1, D), lambda i: (0, 0)),
                  pl.BlockSpec((pl.Element(n_gate), pl.Element(D)), lambda i: (gate_row0, 0))],
        out_specs=[pl.BlockSpec((tr, D), lambda i: (i, 0)),
                   pl.BlockSpec((tr, n_gate), lambda i: (i, 0))],
        out_shape=[jax.ShapeDtypeStruct((M, D), BF16),
                   jax.ShapeDtypeStruct((M, n_gate), F32)],
        compiler_params=_params("parallel"),
        name="prenorm_gates",
    )(x, w.reshape(1, D), wt)


def _post_kernel(y_ref, x_ref, wpost_ref, xo_ref):
    xo_ref[...] = x_ref[...] + _rms(y_ref[...].astype(F32), wpost_ref[...])


def post_norm_residual(y, x, w_post, tr=512):
    M, D = x.shape
    tr = min(tr, M)
    row = pl.BlockSpec((tr, D), lambda i: (i, 0))
    return pl.pallas_call(
        _post_kernel, grid=(M // tr,),
        in_specs=[row, row, pl.BlockSpec((1, D), lambda i: (0, 0))], out_specs=row,
        out_shape=jax.ShapeDtypeStruct((M, D), F32),
        compiler_params=_params("parallel"), name="post_norm",
    )(y, x, w_post.reshape(1, D))


def _weight_spec(w, lead, k_rows, tn, index_map):
    assert w.ndim == len(lead) + 2
    return pl.BlockSpec((None,) * len(lead) + (k_rows, tn),
                        lambda *g: tuple(lead) + index_map(*g))


def _mm_kernel(a_ref, w_ref, o_ref):
    o_ref[...] = jnp.dot(a_ref[...], w_ref[...].astype(BF16),
                         preferred_element_type=F32).astype(o_ref.dtype)


def matmul(a, w, out_dtype, tm, tn, lead=()):
    M, K = a.shape
    N = w.shape[-1]
    assert w.shape[-2] == K and M % tm == 0 and N % tn == 0
    return pl.pallas_call(
        _mm_kernel,
        grid=(N // tn, M // tm),
        in_specs=[pl.BlockSpec((tm, K), lambda j, i: (i, 0)),
                  _weight_spec(w, lead, K, tn, lambda j, i: (0, j))],
        out_specs=pl.BlockSpec((tm, tn), lambda j, i: (i, j)),
        out_shape=jax.ShapeDtypeStruct((M, N), out_dtype),
        compiler_params=_params("parallel", "parallel"),
        name="matmul",
    )(a, w)


def _mm_ksplit_kernel(a_ref, w_ref, o_ref, acc_ref):
    d = jnp.dot(a_ref[...], w_ref[...], preferred_element_type=F32)
    k = pl.program_id(2)
    last = pl.num_programs(2) - 1

    @pl.when(k == 0)
    def _():
        acc_ref[...] = d

    @pl.when((k > 0) & (k < last))
    def _():
        acc_ref[...] += d

    @pl.when(k == last)
    def _():
        o_ref[...] = (acc_ref[...] + d).astype(o_ref.dtype)


def matmul_ksplit(a, w, out_dtype, tm, tn, tk):
    M, K = a.shape
    N = w.shape[1]
    assert M % tm == 0 and N % tn == 0 and K % tk == 0 and K // tk >= 2
    return pl.pallas_call(
        _mm_ksplit_kernel,
        grid=(N // tn, M // tm, K // tk),
        in_specs=[pl.BlockSpec((tm, tk), lambda j, i, k: (i, k)),
                  pl.BlockSpec((tk, tn), lambda j, i, k: (k, j))],
        out_specs=pl.BlockSpec((tm, tn), lambda j, i, k: (i, j)),
        out_shape=jax.ShapeDtypeStruct((M, N), out_dtype),
        scratch_shapes=[pltpu.VMEM((tm, tn), F32)],
        compiler_params=_params("parallel", "parallel", "arbitrary"),
        name="matmul_ksplit",
    )(a, w)


def _side_cast_specs(src, lead, n_steps, step_of):
    assert src.ndim == len(lead) + 2
    R, C = src.shape[-2:]
    pack = 2 * SUBLANES
    rows = _ceil_div(_ceil_div(R, n_steps), pack) * pack
    n_blk = _ceil_div(R, rows)
    idx = lambda *g: (jnp.minimum(step_of(*g), n_blk - 1), 0)
    in_spec = pl.BlockSpec((None,) * len(lead) + (rows, C),
                           lambda *g: tuple(lead) + idx(*g))
    return in_spec, pl.BlockSpec((rows, C), idx), jax.ShapeDtypeStruct((R, C), BF16)


def _side_casts(sides, n_steps, step_of):
    specs = [_side_cast_specs(src, lead, n_steps, step_of) for src, lead in sides]
    return ([s[0] for s in specs], [s[1] for s in specs], [s[2] for s in specs],
            [src for src, _ in sides])


def _round_sides(src_refs, dst_refs):
    for src_ref, dst_ref in zip(src_refs, dst_refs):
        dst_ref[...] = src_ref[...].astype(dst_ref.dtype)


def _in_proj_kernel(a_ref, wt_ref, *rest):
    n_side = len(rest) // 2
    o_ref = rest[n_side]
    _round_sides(rest[:n_side], rest[n_side + 1:])
    o_ref[...] = lax.dot_general(a_ref[...], wt_ref[...].astype(BF16),
                                 (((1,), (1,)), ((), ())),
                                 preferred_element_type=F32).astype(o_ref.dtype)


def input_projection(a, wt, row0, gate0, n_gate, n_cols, tm, tn, sides):
    M, K = a.shape
    assert wt.shape[1] == K and M % tm == 0 and n_cols % tn == 0 and gate0 % tn == 0
    assert row0 % SUBLANES == 0 and n_gate % SUBLANES == 0
    assert row0 + n_cols + n_gate <= wt.shape[0]
    n_m, n_n = M // tm, n_cols // tn
    jgate = gate0 // tn

    def w_rows(i, j):
        r = row0 + j * tn + jnp.where(j >= jgate, n_gate, 0)
        return (pl.multiple_of(r, SUBLANES), 0)

    side_in, side_out, side_shapes, side_srcs = _side_casts(
        sides, n_m * n_n, lambda i, j: i * n_n + j)
    return pl.pallas_call(
        _in_proj_kernel,
        grid=(n_m, n_n),
        in_specs=[pl.BlockSpec((tm, K), lambda i, j: (i, 0), pipeline_mode=pl.Buffered(1)),
                  pl.BlockSpec((pl.Element(tn), pl.Element(K)), w_rows)] + side_in,
        out_specs=[pl.BlockSpec((tm, tn), lambda i, j: (i, j))] + side_out,
        out_shape=[jax.ShapeDtypeStruct((M, n_cols), BF16)] + side_shapes,
        compiler_params=pltpu.CompilerParams(
            dimension_semantics=("arbitrary", "arbitrary"),
            vmem_limit_bytes=VMEM_LIMIT_LARGE_BYTES),
        name="input_projection",
    )(a, wt, *side_srcs)


def _mm_postnorm_kernel(a_ref, w_ref, x_ref, wpost_ref, wpre_ref, xo_ref, h_ref, y_scr,
                        *, n_m):
    i = pl.program_id(0)
    j = pl.program_id(1)
    n_n, _, tn = y_scr.shape[1:]
    rc, d = x_ref.shape

    n_sub = rc // NORM_ROWS
    mm_rows = a_ref.shape[0] // n_sub

    def mm(s):
        rr = slice(s * mm_rows, (s + 1) * mm_rows)
        y_scr[i % 2, j, rr, :] = jnp.dot(a_ref[rr, :], w_ref[...].astype(BF16),
                                         preferred_element_type=F32).astype(y_scr.dtype)

    def norm(s):
        slot = (i + 1) % 2
        cols = [slice(n * tn, (n + 1) * tn) for n in range(n_n)]
        rs = slice(s * NORM_ROWS, (s + 1) * NORM_ROWS)
        rows = pl.ds(pl.multiple_of(j * rc, rc) + s * NORM_ROWS, NORM_ROWS)
        y_tile = lambda n: y_scr[slot, n, rows, :].astype(F32)
        ssq = None
        for n in range(n_n):
            y = y_tile(n)
            ssq = y * y if ssq is None else ssq + y * y
        r_post = lax.rsqrt(jnp.sum(ssq, axis=-1, keepdims=True) / d + EPS)
        ssq = None
        for n, c in enumerate(cols):
            xn = x_ref[rs, c] + y_tile(n) * r_post * wpost_ref[:, c]
            xo_ref[rs, c] = xn
            ssq = xn * xn if ssq is None else ssq + xn * xn
        r_pre = lax.rsqrt(jnp.sum(ssq, axis=-1, keepdims=True) / d + EPS)
        for c in cols:
            h_ref[rs, c] = (xo_ref[rs, c] * r_pre * wpre_ref[:, c]).astype(h_ref.dtype)

    @pl.when(i == 0)
    def _():
        for s in range(n_sub):
            mm(s)

    @pl.when((i > 0) & (i < n_m))
    def _():
        for s in range(n_sub):
            norm(s)
            mm(s)

    @pl.when(i == n_m)
    def _():
        for s in range(n_sub):
            norm(s)


def matmul_post_pre_norm(a, w, x, w_post, w_pre, tm, tn):
    M, K = a.shape
    D = w.shape[1]
    n_m, n_n = M // tm, D // tn
    rc = tm // n_n
    assert M % tm == 0 and D % tn == 0 and tm % n_n == 0
    assert rc % NORM_ROWS == 0 and tm % (rc // NORM_ROWS) == 0
    chunk = pl.BlockSpec(
        (rc, D), lambda i, j: (jnp.maximum(i - 1, 0) * n_n + jnp.where(i == 0, 0, j), 0))
    vec = pl.BlockSpec((1, D), lambda i, j: (0, 0))
    return pl.pallas_call(
        functools.partial(_mm_postnorm_kernel, n_m=n_m),
        grid=(n_m + 1, n_n),
        in_specs=[pl.BlockSpec((tm, K), lambda i, j: (jnp.minimum(i, n_m - 1), 0)),
                  pl.BlockSpec((K, tn), lambda i, j: (0, jnp.where(i == n_m, n_n - 1, j))),
                  chunk, vec, vec],
        out_specs=[chunk, chunk],
        out_shape=[jax.ShapeDtypeStruct((M, D), F32), jax.ShapeDtypeStruct((M, D), BF16)],
        scratch_shapes=[pltpu.VMEM((2, n_n, tm, tn), BF16)],
        compiler_params=_params("arbitrary", "arbitrary"),
        name="matmul_post_pre_norm",
    )(a, w, x, w_post.reshape(1, D), w_pre.reshape(1, D))


def _swiglu_kernel(a_ref, wg_ref, wu_ref, src_ref, o_ref, dst_ref):
    dst_ref[...] = src_ref[...].astype(dst_ref.dtype)
    wg = wg_ref[...]
    wu = wu_ref[...]
    for r0 in range(0, o_ref.shape[0], EPILOGUE_ROWS):
        rs = slice(r0, r0 + EPILOGUE_ROWS)
        a = a_ref[rs, :]
        g = jnp.dot(a, wg, preferred_element_type=F32)
        u = jnp.dot(a, wu, preferred_element_type=F32)
        o_ref[rs, :] = (g * jax.nn.sigmoid(g) * u).astype(o_ref.dtype)


def swiglu_up(a, wg, wu, tm, tn, side_src, side_lead):
    M, K = a.shape
    N = wg.shape[-1]
    assert M % tm == 0 and N % tn == 0
    n_m, n_n = M // tm, N // tn
    wspec = _weight_spec(wg, (), K, tn, lambda i, j: (0, j))
    side_in, side_out, side_shape = _side_cast_specs(
        side_src, side_lead, n_m * n_n, lambda i, j: i * n_n + j)
    return pl.pallas_call(
        _swiglu_kernel,
        grid=(n_m, n_n),
        in_specs=[pl.BlockSpec((tm, K), lambda i, j: (i, 0), pipeline_mode=pl.Buffered(1)),
                  wspec, wspec, side_in],
        out_specs=[pl.BlockSpec((tm, tn), lambda i, j: (i, j)), side_out],
        out_shape=[jax.ShapeDtypeStruct((M, N), BF16), side_shape],
        compiler_params=_params("arbitrary", "arbitrary"),
        name="swiglu_up",
    )(a, wg, wu, side_src)


def _branch_kernel(a0_ref, a1_ref, w0_ref, w1_ref, g0_ref, g1_ref, o_ref):
    w0 = w0_ref[...].astype(BF16)
    w1 = w1_ref[...].astype(BF16)
    for r0 in range(0, o_ref.shape[0], EPILOGUE_ROWS):
        rs = slice(r0, r0 + EPILOGUE_ROWS)
        y0 = jnp.dot(a0_ref[rs, :], w0, preferred_element_type=F32)
        y1 = jnp.dot(a1_ref[rs, :], w1, preferred_element_type=F32)
        o_ref[rs, :] = (jax.nn.sigmoid(g0_ref[rs, :].astype(F32)) * y0
                        + jax.nn.sigmoid(g1_ref[rs, :].astype(F32)) * y1).astype(o_ref.dtype)


def branch_merge(a0, a1, w, lead, proj, g_col0, tm, tn):
    M, K = a0.shape
    N = w.shape[-1]
    assert M % tm == 0 and N % tn == 0 and g_col0 % tn == 0
    gb0 = g_col0 // tn
    gb1 = (g_col0 + N) // tn
    aspec = pl.BlockSpec((tm, K), lambda j, i: (i, 0))
    return pl.pallas_call(
        _branch_kernel,
        grid=(N // tn, M // tm),
        in_specs=[aspec, aspec,
                  _weight_spec(w, tuple(lead) + (0,), K, tn, lambda j, i: (0, j)),
                  _weight_spec(w, tuple(lead) + (1,), K, tn, lambda j, i: (0, j)),
                  pl.BlockSpec((tm, tn), lambda j, i: (i, gb0 + j)),
                  pl.BlockSpec((tm, tn), lambda j, i: (i, gb1 + j))],
        out_specs=pl.BlockSpec((tm, tn), lambda j, i: (i, j)),
        out_shape=jax.ShapeDtypeStruct((M, N), BF16),
        compiler_params=_params("parallel", "parallel"),
        name="branch_merge",
    )(a0, a1, w, w, proj, proj)


def _mlstm_kernel(bias_ref, q_ref, k_ref, v_ref, o_ref, gt_ref, hn_ref, *rest):
    n_side = (len(rest) - 4) // 2
    out_ref = rest[n_side]
    c_ref, n_ref, m_ref = rest[2 * n_side + 1:]
    _round_sides(rest[:n_side], rest[n_side + 1:2 * n_side + 1])
    H = MLSTM_HEADS
    nb, L = q_ref.shape[:2]
    dqk = q_ref.shape[2] // H
    dv = v_ref.shape[2] // H
    scale = dqk ** -0.5
    streams = range(nb * H)
    qs = lambda s: (s // H, slice(None), slice((s % H) * dqk, (s % H + 1) * dqk))
    vs = lambda s: (s // H, slice(None), slice((s % H) * dv, (s % H + 1) * dv))
    hv = lambda s: (slice(None), slice((s % H) * dv, (s % H + 1) * dv))

    @pl.when(pl.program_id(1) == 0)
    def _():
        c_ref[...] = jnp.zeros_like(c_ref)
        n_ref[...] = jnp.zeros_like(n_ref)
        m_ref[...] = jnp.zeros_like(m_ref)

    li, b_row = [], []
    for b in range(nb):
        pre = GATE_CAP * jnp.tanh((gt_ref[b] + bias_ref[...]) / GATE_CAP)
        lf = jnp.minimum(pre, 0.0) - jnp.log1p(jnp.exp(-jnp.abs(pre)))
        lane = lax.broadcasted_iota(jnp.int32, lf.shape, 1)
        b_all = lf
        sh = 1
        while sh < L:
            b_all = b_all + jnp.where(lane >= sh, pltpu.roll(b_all, sh, axis=1), 0.0)
            sh *= 2
        li += [pre[h:h + 1, :] for h in range(H)]
        b_row += [b_all[H + h:H + h + 1, :] for h in range(H)]
    m_prev = [m_ref[s][:, 0:1] for s in streams]

    row = lax.broadcasted_iota(jnp.int32, (L, L), 0)
    col = lax.broadcasted_iota(jnp.int32, (L, L), 1)
    b_sq = [jnp.broadcast_to(b_row[s], (L, L)).T for s in streams]
    dm = [jnp.where(col <= row, b_sq[s] - b_row[s] + li[s], -jnp.inf) for s in streams]
    inter = [b_sq[s][:, 0:1] + m_prev[s] for s in streams]
    m_row = [jnp.maximum(inter[s], jnp.max(dm[s], axis=-1, keepdims=True)) for s in streams]
    w_intra = [jnp.exp(dm[s] - m_row[s]) for s in streams]
    s_inter = [jnp.exp(inter[s] - m_row[s]) for s in streams]

    b_last = [b_row[s][:, L - 1:L] for s in streams]
    a_row = [b_last[s] - b_row[s] + li[s] for s in streams]
    m_new = [jnp.maximum(b_last[s] + m_prev[s], jnp.max(a_row[s], axis=-1, keepdims=True))
             for s in streams]
    s_state = [jnp.exp(b_last[s] + m_prev[s] - m_new[s]) for s in streams]
    e_col = [jnp.broadcast_to(jnp.exp(a_row[s] - m_new[s]), (L, L)).T[:, 0:1]
             for s in streams]

    nt = (((1,), (1,)), ((), ()))
    c_old = [c_ref[s] for s in streams]
    n_old = [n_ref[s] for s in streams]
    qk = [lax.dot_general(q_ref[qs(s)], k_ref[qs(s)], nt, preferred_element_type=F32)
          * scale * w_intra[s] for s in streams]
    q_c = [jnp.dot(q_ref[qs(s)], c_old[s].astype(BF16), preferred_element_type=F32) * scale
           for s in streams]
    num = [s_inter[s] * q_c[s] + jnp.dot(qk[s].astype(BF16), v_ref[vs(s)],
                                         preferred_element_type=F32) for s in streams]
    q_n = [jnp.sum(q_ref[qs(s)].astype(F32) * n_old[s], axis=-1, keepdims=True) * scale
           for s in streams]
    den = [s_inter[s] * q_n[s] + jnp.sum(qk[s], axis=-1, keepdims=True) for s in streams]
    for s in streams:
        hh = num[s] / jnp.maximum(jnp.abs(den[s]), jnp.exp(-m_row[s]))
        out_ref[vs(s)] = (jax.nn.sigmoid(o_ref[vs(s)].astype(F32))
                          * _rms(hh, hn_ref[hv(s)])).astype(out_ref.dtype)

    for s in streams:
        kw = k_ref[qs(s)].astype(F32) * e_col[s]
        c_ref[s] = s_state[s] * c_old[s] + lax.dot_general(
            kw.astype(BF16), v_ref[vs(s)], (((0,), (0,)), ((), ())),
            preferred_element_type=F32)
        n_ref[s] = s_state[s] * n_old[s] + jnp.sum(kw, axis=0, keepdims=True)
        m_ref[s] = jnp.broadcast_to(m_new[s], m_ref.shape[1:])


def mlstm_mixer(proj, gates, b_if, head_norm, batch, seq, q_col0, k_col0, v_col0, o_col0,
                dqk, dv, sides):
    L = MLSTM_CHUNK
    T = seq // L
    H = MLSTM_HEADS
    nb = 1
    qkw, vw = H * dqk, H * dv
    assert q_col0 % qkw == 0 and k_col0 % qkw == 0 and v_col0 % vw == 0 and o_col0 % vw == 0
    qb, kb, vb, ob = q_col0 // qkw, k_col0 // qkw, v_col0 // vw, o_col0 // vw
    proj3 = proj.reshape(batch, seq, proj.shape[1])
    gates_t = jnp.swapaxes(gates.reshape(batch, seq, 2 * H), 1, 2)
    rows = lambda cb: (lambda g, t: (g, t, cb))
    n_g = batch // nb
    side_in, side_out, side_shapes, side_srcs = _side_casts(
        sides, n_g * T, lambda g, t: g * T + t)
    out, *rounded = pl.pallas_call(
        _mlstm_kernel,
        grid=(n_g, T),
        in_specs=[
            pl.BlockSpec((2 * H, 1), lambda g, t: (0, 0)),
            pl.BlockSpec((nb, L, qkw), rows(qb)),
            pl.BlockSpec((nb, L, qkw), rows(kb)),
            pl.BlockSpec((nb, L, vw), rows(vb)),
            pl.BlockSpec((nb, L, vw), rows(ob)),
            pl.BlockSpec((nb, 2 * H, L), lambda g, t: (g, 0, t)),
            pl.BlockSpec((1, vw), lambda g, t: (0, 0)),
        ] + side_in,
        out_specs=[pl.BlockSpec((nb, L, vw), rows(0))] + side_out,
        out_shape=[jax.ShapeDtypeStruct((batch, seq, vw), BF16)] + side_shapes,
        scratch_shapes=[pltpu.VMEM((nb * H, dqk, dv), F32),
                        pltpu.VMEM((nb * H, 1, dqk), F32),
                        pltpu.VMEM((nb * H, 1, LANES), F32)],
        compiler_params=_params("arbitrary", "arbitrary"),
        name="mlstm",
    )(b_if.reshape(2 * H, 1), proj3, proj3, proj3, proj3, gates_t, head_norm.reshape(1, vw),
      *side_srcs)
    return (out.reshape(batch * seq, vw), *rounded)


def _conv_kernel(cb_ref, cc_ref, cx_ref, w_ref, o_ref, carry_ref):
    s = pl.program_id(2)
    ts = cb_ref.shape[0]

    @pl.when(s == 0)
    def _():
        carry_ref[...] = jnp.zeros_like(carry_ref)

    w = w_ref[...]
    w0, w1, w2 = w[0:1, :], w[1:2, :], w[2:3, :]
    u = cc_ref[...].astype(F32) * cx_ref[...].astype(F32)
    y = w2 * u + w1 * pltpu.roll(u, 1, axis=0) + w0 * pltpu.roll(u, 2, axis=0)
    o_ref[...] = (cb_ref[...].astype(F32) * y).astype(o_ref.dtype)

    prev = carry_ref[...]
    u_h = u[0:8, :]
    r = lax.broadcasted_iota(jnp.int32, u_h.shape, 0)
    u1 = jnp.where(r < 1, pltpu.roll(prev, 1, axis=0), pltpu.roll(u_h, 1, axis=0))
    u2 = jnp.where(r < 2, pltpu.roll(prev, 2, axis=0), pltpu.roll(u_h, 2, axis=0))
    y_h = w2 * u_h + w1 * u1 + w0 * u2
    o_ref[0:8, :] = (cb_ref[0:8, :].astype(F32) * y_h).astype(o_ref.dtype)
    carry_ref[...] = u[ts - 8:ts, :]


def gated_conv(proj, conv_w, batch, seq, cb_col0, cc_col0, cx_col0, ts=512, tc=512):
    C = conv_w.shape[1]
    nS = seq // ts
    b0, c0, x0 = cb_col0 // tc, cc_col0 // tc, cx_col0 // tc
    return pl.pallas_call(
        _conv_kernel,
        grid=(batch, C // tc, nS),
        in_specs=[
            pl.BlockSpec((ts, tc), lambda b, c, s: (b * nS + s, b0 + c)),
            pl.BlockSpec((ts, tc), lambda b, c, s: (b * nS + s, c0 + c)),
            pl.BlockSpec((ts, tc), lambda b, c, s: (b * nS + s, x0 + c)),
            pl.BlockSpec((CONV_K, tc), lambda b, c, s: (0, c)),
        ],
        out_specs=pl.BlockSpec((ts, tc), lambda b, c, s: (b * nS + s, c)),
        out_shape=jax.ShapeDtypeStruct((batch * seq, C), BF16),
        scratch_shapes=[pltpu.VMEM((8, tc), F32)],
        compiler_params=_params("parallel", "parallel", "arbitrary"),
        name="gated_conv",
    )(proj, proj, proj, conv_w)


def _xq_attn_kernel(a_ref, wq_ref, k_ref, v_ref, src_ref, o_ref, dst_ref):
    dst_ref[...] = src_ref[...].astype(dst_ref.dtype)
    dh = v_ref.shape[1]
    q = jnp.dot(a_ref[...], wq_ref[...], preferred_element_type=F32).astype(BF16)
    s = lax.dot_general(q, k_ref[...], (((1,), (1,)), ((), ())),
                        preferred_element_type=F32) * (dh ** -0.5)
    e = jnp.exp(s - jnp.max(s, axis=-1, keepdims=True))
    p = e / jnp.sum(e, axis=-1, keepdims=True)
    o_ref[...] = jnp.dot(p.astype(BF16), v_ref[...],
                         preferred_element_type=F32).astype(o_ref.dtype)


def xq_cross_attention(a, w_xq, k, v, seq, mem_len, dh, tm, side_src, side_lead):
    M, K = a.shape
    H = XATTN_HEADS
    assert M % tm == 0 and seq % tm == 0 and w_xq.shape == (K, H * dh)
    tiles_per_batch = seq // tm
    n_m = M // tm
    side_in, side_out, side_shape = _side_cast_specs(
        side_src, side_lead, n_m * H, lambda i, j: i * H + j)
    kv_spec = pl.BlockSpec((mem_len, dh), lambda i, j: (i // tiles_per_batch, j))
    return pl.pallas_call(
        _xq_attn_kernel,
        grid=(n_m, H),
        in_specs=[pl.BlockSpec((tm, K), lambda i, j: (i, 0)),
                  pl.BlockSpec((K, dh), lambda i, j: (0, j)),
                  kv_spec, kv_spec, side_in],
        out_specs=[pl.BlockSpec((tm, dh), lambda i, j: (i, j)), side_out],
        out_shape=[jax.ShapeDtypeStruct((M, H * dh), BF16), side_shape],
        compiler_params=_params("arbitrary", "arbitrary"),
        name="xq_cross_attention",
    )(a, w_xq, k, v, side_src)


def kernel(x, mem, norm_pre_mix, norm_post_mix, w_in, b_if, mlstm_head_norm, conv_w, w_branch, w_mix_out, norm_pre_xattn, norm_post_xattn, norm_mem, w_xq, w_xk, w_xv, w_xo, norm_pre_ffn, norm_post_ffn, w_ffn_gate, w_ffn_up, w_ffn_down):
    B, S, D = x.shape
    mem_len = mem.shape[1]
    depth = w_in.shape[0]
    H = MLSTM_HEADS
    v_width = mlstm_head_norm.shape[1]
    conv_width = conv_w.shape[2]
    dv = v_width // H
    dqk = dv // 2
    qk_width = H * dqk
    dh = w_xq.shape[2] // XATTN_HEADS

    in_width = w_in.shape[2]
    w_in_t = jnp.swapaxes(w_in, 1, 2).reshape(depth * in_width, D)
    q0 = 0
    k0 = q0 + qk_width
    v0 = k0 + qk_width
    o0 = v0 + v_width
    gate0 = o0 + v_width
    cb0 = gate0
    cc0 = cb0 + conv_width
    cx0 = cc0 + conv_width
    g0 = cx0 + conv_width

    xr = x.reshape(B * S, D)
    memr = mem.reshape(B * mem_len, D)
    for l in range(depth):
        h, gates = prenorm_gates(xr, norm_pre_mix[l], w_in_t, l * in_width + gate0, 2 * H)
        proj, w_mix_bf, w_xq_bf, w_gate_bf = input_projection(
            h, w_in_t, l * in_width, gate0, 2 * H, in_width - 2 * H, 1024, 1024,
            [(w_mix_out, (l,)), (w_xq, (l,)), (w_ffn_gate, (l,))])
        h_a, w_up_bf = mlstm_mixer(proj, gates, b_if[l], mlstm_head_norm[l], B, S,
                                   q0, k0, v0, o0, dqk, dv, [(w_ffn_up, (l,))])
        h_b = gated_conv(proj, conv_w[l], B, S, cb0, cc0, cx0, ts=1024, tc=1024)
        merged = branch_merge(h_a, h_b, w_branch, (l,), proj, g0, 1024, 512)
        xr, h = matmul_post_pre_norm(merged, w_mix_bf, xr,
                                     norm_post_mix[l], norm_pre_xattn[l], 1024, 512)

        m = prenorm(memr, norm_mem[l])
        k = matmul(m, w_xk, BF16, B * mem_len, 1024, lead=(l,))
        v = matmul(m, w_xv, BF16, B * mem_len, 1024, lead=(l,))
        o, w_xo_bf = xq_cross_attention(h, w_xq_bf, k, v, S, mem_len, dh, 1024, w_xo, (l,))
        xr, h = matmul_post_pre_norm(o, w_xo_bf, xr,
                                     norm_post_xattn[l], norm_pre_ffn[l], 1024, 512)

        hidden, w_down_bf = swiglu_up(h, w_gate_bf, w_up_bf, 4096, 256, w_ffn_down, (l,))
        y = matmul_ksplit(hidden, w_down_bf, BF16, 1024, 1024, w_down_bf.shape[0] // 2)
        xr = post_norm_residual(y, xr, norm_post_ffn[l])
    return xr.reshape(B, S, D)
```

```python
import functools

import jax
import jax.numpy as jnp
from jax import lax
from jax.experimental import pallas as pl
from jax.experimental.pallas import tpu as pltpu

F32 = jnp.float32
BF16 = jnp.bfloat16

MLSTM_HEADS = 4
XATTN_HEADS = 4
CONV_K = 3
GATE_CAP = 15.0
EPS = 1e-6
MLSTM_CHUNK = 256
LANES = 128
SUBLANES = 8
NORM_ROWS = 32
EPILOGUE_ROWS = 256

VMEM_LIMIT_BYTES = 56 * 1024 * 1024
VMEM_LIMIT_LARGE_BYTES = 60 * 1024 * 1024


def _params(*sem):
    return pltpu.CompilerParams(dimension_semantics=sem,
                                vmem_limit_bytes=VMEM_LIMIT_BYTES)


def _ceil_div(a, b):
    return -(-a // b)


def _rms(x, w):
    return x * lax.rsqrt(jnp.mean(x * x, axis=-1, keepdims=True) + EPS) * w


def _prenorm_kernel(x_ref, w_ref, h_ref):
    h_ref[...] = _rms(x_ref[...].astype(F32), w_ref[...]).astype(h_ref.dtype)


def prenorm(x, w, tr=512):
    M, D = x.shape
    tr = min(tr, M)
    return pl.pallas_call(
        _prenorm_kernel,
        grid=(M // tr,),
        in_specs=[pl.BlockSpec((tr, D), lambda i: (i, 0)),
                  pl.BlockSpec((1, D), lambda i: (0, 0))],
        out_specs=pl.BlockSpec((tr, D), lambda i: (i, 0)),
        out_shape=jax.ShapeDtypeStruct((M, D), BF16),
        compiler_params=_params("parallel"),
        name="prenorm",
    )(x, w.reshape(1, D))


def _post_kernel(y_ref, x_ref, wpost_ref, xo_ref):
    xo_ref[...] = x_ref[...] + _rms(y_ref[...].astype(F32), wpost_ref[...])


def post_norm_residual(y, x, w_post, tr=512):
    M, D = x.shape
    tr = min(tr, M)
    row = pl.BlockSpec((tr, D), lambda i: (i, 0))
    return pl.pallas_call(
        _post_kernel, grid=(M // tr,),
        in_specs=[row, row, pl.BlockSpec((1, D), lambda i: (0, 0))], out_specs=row,
        out_shape=jax.ShapeDtypeStruct((M, D), F32),
        compiler_params=_params("parallel"), name="post_norm",
    )(y, x, w_post.reshape(1, D))


def _weight_spec(w, lead, k_rows, tn, index_map):
    assert w.ndim == len(lead) + 2
    return pl.BlockSpec((None,) * len(lead) + (k_rows, tn),
                        lambda *g: tuple(lead) + index_map(*g))


def _mm_kernel(a_ref, w_ref, o_ref):
    o_ref[...] = jnp.dot(a_ref[...], w_ref[...].astype(BF16),
                         preferred_element_type=F32).astype(o_ref.dtype)


def matmul(a, w, out_dtype, tm, tn, lead=()):
    M, K = a.shape
    N = w.shape[-1]
    assert w.shape[-2] == K and M % tm == 0 and N % tn == 0
    return pl.pallas_call(
        _mm_kernel,
        grid=(N // tn, M // tm),
        in_specs=[pl.BlockSpec((tm, K), lambda j, i: (i, 0)),
                  _weight_spec(w, lead, K, tn, lambda j, i: (0, j))],
        out_specs=pl.BlockSpec((tm, tn), lambda j, i: (i, j)),
        out_shape=jax.ShapeDtypeStruct((M, N), out_dtype),
        compiler_params=_params("parallel", "parallel"),
        name="matmul",
    )(a, w)


def _mm_ksplit_kernel(a_ref, w_ref, o_ref, acc_ref):
    d = jnp.dot(a_ref[...], w_ref[...], preferred_element_type=F32)
    k = pl.program_id(2)
    last = pl.num_programs(2) - 1

    @pl.when(k == 0)
    def _():
        acc_ref[...] = d

    @pl.when((k > 0) & (k < last))
    def _():
        acc_ref[...] += d

    @pl.when(k == last)
    def _():
        o_ref[...] = (acc_ref[...] + d).astype(o_ref.dtype)


def matmul_ksplit(a, w, out_dtype, tm, tn, tk):
    M, K = a.shape
    N = w.shape[1]
    assert M % tm == 0 and N % tn == 0 and K % tk == 0 and K // tk >= 2
    return pl.pallas_call(
        _mm_ksplit_kernel,
        grid=(N // tn, M // tm, K // tk),
        in_specs=[pl.BlockSpec((tm, tk), lambda j, i, k: (i, k)),
                  pl.BlockSpec((tk, tn), lambda j, i, k: (k, j))],
        out_specs=pl.BlockSpec((tm, tn), lambda j, i, k: (i, j)),
        out_shape=jax.ShapeDtypeStruct((M, N), out_dtype),
        scratch_shapes=[pltpu.VMEM((tm, tn), F32)],
        compiler_params=_params("parallel", "parallel", "arbitrary"),
        name="matmul_ksplit",
    )(a, w)


def _side_cast_specs(src, lead, n_steps, step_of):
    assert src.ndim == len(lead) + 2
    R, C = src.shape[-2:]
    pack = 2 * SUBLANES
    rows = _ceil_div(_ceil_div(R, n_steps), pack) * pack
    n_blk = _ceil_div(R, rows)
    idx = lambda *g: (jnp.minimum(step_of(*g), n_blk - 1), 0)
    in_spec = pl.BlockSpec((None,) * len(lead) + (rows, C),
                           lambda *g: tuple(lead) + idx(*g))
    return in_spec, pl.BlockSpec((rows, C), idx), jax.ShapeDtypeStruct((R, C), BF16)


def _side_casts(sides, n_steps, step_of):
    specs = [_side_cast_specs(src, lead, n_steps, step_of) for src, lead in sides]
    return ([s[0] for s in specs], [s[1] for s in specs], [s[2] for s in specs],
            [src for src, _ in sides])


def _round_sides(src_refs, dst_refs):
    for src_ref, dst_ref in zip(src_refs, dst_refs):
        dst_ref[...] = src_ref[...].astype(dst_ref.dtype)


def _in_proj_kernel(a_ref, wt_ref, wgate_ref, *rest):
    n_side = (len(rest) - 2) // 2
    o_ref, gates_ref = rest[n_side:n_side + 2]
    _round_sides(rest[:n_side], rest[n_side + 2:])
    nt = (((1,), (1,)), ((), ()))
    a = a_ref[...]
    o_ref[...] = lax.dot_general(a, wt_ref[...].astype(BF16), nt,
                                 preferred_element_type=F32).astype(o_ref.dtype)

    @pl.when(pl.program_id(1) == 0)
    def _():
        gates_ref[...] = lax.dot_general(a, wgate_ref[...].astype(BF16), nt,
                                         preferred_element_type=F32)


def input_projection(a, wt, row0, gate0, n_gate, n_cols, tm, tn, sides):
    M, K = a.shape
    assert wt.shape[1] == K and M % tm == 0 and n_cols % tn == 0 and gate0 % tn == 0
    assert row0 % SUBLANES == 0 and n_gate % SUBLANES == 0
    assert row0 + n_cols + n_gate <= wt.shape[0]
    n_m, n_n = M // tm, n_cols // tn
    jgate = gate0 // tn

    def w_rows(i, j):
        r = row0 + j * tn + jnp.where(j >= jgate, n_gate, 0)
        return (pl.multiple_of(r, SUBLANES), 0)

    side_in, side_out, side_shapes, side_srcs = _side_casts(
        sides, n_m * n_n, lambda i, j: i * n_n + j)
    return pl.pallas_call(
        _in_proj_kernel,
        grid=(n_m, n_n),
        in_specs=[pl.BlockSpec((tm, K), lambda i, j: (i, 0), pipeline_mode=pl.Buffered(1)),
                  pl.BlockSpec((pl.Element(tn), pl.Element(K)), w_rows),
                  pl.BlockSpec((pl.Element(n_gate), pl.Element(K)),
                               lambda i, j: (row0 + gate0, 0))] + side_in,
        out_specs=[pl.BlockSpec((tm, tn), lambda i, j: (i, j)),
                   pl.BlockSpec((tm, n_gate), lambda i, j: (i, 0))] + side_out,
        out_shape=[jax.ShapeDtypeStruct((M, n_cols), BF16),
                   jax.ShapeDtypeStruct((M, n_gate), F32)] + side_shapes,
        compiler_params=pltpu.CompilerParams(
            dimension_semantics=("arbitrary", "arbitrary"),
            vmem_limit_bytes=VMEM_LIMIT_LARGE_BYTES),
        name="input_projection",
    )(a, wt, wt, *side_srcs)


def _mm_postnorm_kernel(a_ref, w_ref, x_ref, wpost_ref, wpre_ref, xo_ref, h_ref, y_scr,
                        *, n_m):
    i = pl.program_id(0)
    j = pl.program_id(1)
    n_n, _, tn = y_scr.shape[1:]
    rc, d = x_ref.shape

    n_sub = rc // NORM_ROWS
    mm_rows = a_ref.shape[0] // n_sub

    def mm(s):
        rr = slice(s * mm_rows, (s + 1) * mm_rows)
        y_scr[i % 2, j, rr, :] = jnp.dot(a_ref[rr, :], w_ref[...].astype(BF16),
                                         preferred_element_type=F32).astype(y_scr.dtype)

    def norm(s):
        slot = (i + 1) % 2
        cols = [slice(n * tn, (n + 1) * tn) for n in range(n_n)]
        rs = slice(s * NORM_ROWS, (s + 1) * NORM_ROWS)
        rows = pl.ds(pl.multiple_of(j * rc, rc) + s * NORM_ROWS, NORM_ROWS)
        y_tile = lambda n: y_scr[slot, n, rows, :].astype(F32)
        ssq = None
        for n in range(n_n):
            y = y_tile(n)
            ssq = y * y if ssq is None else ssq + y * y
        r_post = lax.rsqrt(jnp.sum(ssq, axis=-1, keepdims=True) / d + EPS)
        ssq = None
        for n, c in enumerate(cols):
            xn = x_ref[rs, c] + y_tile(n) * r_post * wpost_ref[:, c]
            xo_ref[rs, c] = xn
            ssq = xn * xn if ssq is None else ssq + xn * xn
        r_pre = lax.rsqrt(jnp.sum(ssq, axis=-1, keepdims=True) / d + EPS)
        for c in cols:
            h_ref[rs, c] = (xo_ref[rs, c] * r_pre * wpre_ref[:, c]).astype(h_ref.dtype)

    @pl.when(i == 0)
    def _():
        for s in range(n_sub):
            mm(s)

    @pl.when((i > 0) & (i < n_m))
    def _():
        for s in range(n_sub):
            norm(s)
            mm(s)

    @pl.when(i == n_m)
    def _():
        for s in range(n_sub):
            norm(s)


def matmul_post_pre_norm(a, w, x, w_post, w_pre, tm, tn):
    M, K = a.shape
    D = w.shape[1]
    n_m, n_n = M // tm, D // tn
    rc = tm // n_n
    assert M % tm == 0 and D % tn == 0 and tm % n_n == 0
    assert rc % NORM_ROWS == 0 and tm % (rc // NORM_ROWS) == 0
    chunk = pl.BlockSpec(
        (rc, D), lambda i, j: (jnp.maximum(i - 1, 0) * n_n + jnp.where(i == 0, 0, j), 0))
    vec = pl.BlockSpec((1, D), lambda i, j: (0, 0))
    return pl.pallas_call(
        functools.partial(_mm_postnorm_kernel, n_m=n_m),
        grid=(n_m + 1, n_n),
        in_specs=[pl.BlockSpec((tm, K), lambda i, j: (jnp.minimum(i, n_m - 1), 0)),
                  pl.BlockSpec((K, tn), lambda i, j: (0, jnp.where(i == n_m, n_n - 1, j))),
                  chunk, vec, vec],
        out_specs=[chunk, chunk],
        out_shape=[jax.ShapeDtypeStruct((M, D), F32), jax.ShapeDtypeStruct((M, D), BF16)],
        scratch_shapes=[pltpu.VMEM((2, n_n, tm, tn), BF16)],
        compiler_params=_params("arbitrary", "arbitrary"),
        name="matmul_post_pre_norm",
    )(a, w, x, w_post.reshape(1, D), w_pre.reshape(1, D))


def _swiglu_kernel(a_ref, wg_ref, wu_ref, src_ref, o_ref, dst_ref):
    dst_ref[...] = src_ref[...].astype(dst_ref.dtype)
    wg = wg_ref[...]
    wu = wu_ref[...]
    for r0 in range(0, o_ref.shape[0], EPILOGUE_ROWS):
        rs = slice(r0, r0 + EPILOGUE_ROWS)
        a = a_ref[rs, :]
        g = jnp.dot(a, wg, preferred_element_type=F32)
        u = jnp.dot(a, wu, preferred_element_type=F32)
        o_ref[rs, :] = (g * jax.nn.sigmoid(g) * u).astype(o_ref.dtype)


def swiglu_up(a, wg, wu, tm, tn, side_src, side_lead):
    M, K = a.shape
    N = wg.shape[-1]
    assert M % tm == 0 and N % tn == 0
    n_m, n_n = M // tm, N // tn
    wspec = _weight_spec(wg, (), K, tn, lambda i, j: (0, j))
    side_in, side_out, side_shape = _side_cast_specs(
        side_src, side_lead, n_m * n_n, lambda i, j: i * n_n + j)
    return pl.pallas_call(
        _swiglu_kernel,
        grid=(n_m, n_n),
        in_specs=[pl.BlockSpec((tm, K), lambda i, j: (i, 0), pipeline_mode=pl.Buffered(1)),
                  wspec, wspec, side_in],
        out_specs=[pl.BlockSpec((tm, tn), lambda i, j: (i, j)), side_out],
        out_shape=[jax.ShapeDtypeStruct((M, N), BF16), side_shape],
        compiler_params=_params("arbitrary", "arbitrary"),
        name="swiglu_up",
    )(a, wg, wu, side_src)


def _branch_kernel(a0_ref, a1_ref, w0_ref, w1_ref, g0_ref, g1_ref, o_ref):
    w0 = w0_ref[...].astype(BF16)
    w1 = w1_ref[...].astype(BF16)
    for r0 in range(0, o_ref.shape[0], EPILOGUE_ROWS):
        rs = slice(r0, r0 + EPILOGUE_ROWS)
        y0 = jnp.dot(a0_ref[rs, :], w0, preferred_element_type=F32)
        y1 = jnp.dot(a1_ref[rs, :], w1, preferred_element_type=F32)
        o_ref[rs, :] = (jax.nn.sigmoid(g0_ref[rs, :].astype(F32)) * y0
                        + jax.nn.sigmoid(g1_ref[rs, :].astype(F32)) * y1).astype(o_ref.dtype)


def branch_merge(a0, a1, w, lead, proj, g_col0, tm, tn):
    M, K = a0.shape
    N = w.shape[-1]
    assert M % tm == 0 and N % tn == 0 and g_col0 % tn == 0
    gb0 = g_col0 // tn
    gb1 = (g_col0 + N) // tn
    aspec = pl.BlockSpec((tm, K), lambda i, j: (i, 0), pipeline_mode=pl.Buffered(1))
    return pl.pallas_call(
        _branch_kernel,
        grid=(M // tm, N // tn),
        in_specs=[aspec, aspec,
                  _weight_spec(w, tuple(lead) + (0,), K, tn, lambda i, j: (0, j)),
                  _weight_spec(w, tuple(lead) + (1,), K, tn, lambda i, j: (0, j)),
                  pl.BlockSpec((tm, tn), lambda i, j: (i, gb0 + j)),
                  pl.BlockSpec((tm, tn), lambda i, j: (i, gb1 + j))],
        out_specs=pl.BlockSpec((tm, tn), lambda i, j: (i, j)),
        out_shape=jax.ShapeDtypeStruct((M, N), BF16),
        compiler_params=_params("parallel", "parallel"),
        name="branch_merge",
    )(a0, a1, w, w, proj, proj)


def _mlstm_kernel(bias_ref, q_ref, k_ref, v_ref, o_ref, gt_ref, hn_ref, *rest):
    n_side = (len(rest) - 4) // 2
    out_ref = rest[n_side]
    c_ref, n_ref, m_ref = rest[2 * n_side + 1:]
    _round_sides(rest[:n_side], rest[n_side + 1:2 * n_side + 1])
    H = MLSTM_HEADS
    nb, L = q_ref.shape[:2]
    dqk = q_ref.shape[2] // H
    dv = v_ref.shape[2] // H
    scale = dqk ** -0.5
    streams = range(nb * H)
    qs = lambda s: (s // H, slice(None), slice((s % H) * dqk, (s % H + 1) * dqk))
    vs = lambda s: (s // H, slice(None), slice((s % H) * dv, (s % H + 1) * dv))
    hv = lambda s: (slice(None), slice((s % H) * dv, (s % H + 1) * dv))

    @pl.when(pl.program_id(1) == 0)
    def _():
        c_ref[...] = jnp.zeros_like(c_ref)
        n_ref[...] = jnp.zeros_like(n_ref)
        m_ref[...] = jnp.zeros_like(m_ref)

    li, b_row = [], []
    for b in range(nb):
        pre = GATE_CAP * jnp.tanh((gt_ref[b] + bias_ref[...]) / GATE_CAP)
        lf = jnp.minimum(pre, 0.0) - jnp.log1p(jnp.exp(-jnp.abs(pre)))
        lane = lax.broadcasted_iota(jnp.int32, lf.shape, 1)
        b_all = lf
        sh = 1
        while sh < L:
            b_all = b_all + jnp.where(lane >= sh, pltpu.roll(b_all, sh, axis=1), 0.0)
            sh *= 2
        li += [pre[h:h + 1, :] for h in range(H)]
        b_row += [b_all[H + h:H + h + 1, :] for h in range(H)]
    m_prev = [m_ref[s][:, 0:1] for s in streams]

    row = lax.broadcasted_iota(jnp.int32, (L, L), 0)
    col = lax.broadcasted_iota(jnp.int32, (L, L), 1)
    b_sq = [jnp.broadcast_to(b_row[s], (L, L)).T for s in streams]
    dm = [jnp.where(col <= row, b_sq[s] - b_row[s] + li[s], -jnp.inf) for s in streams]
    inter = [b_sq[s][:, 0:1] + m_prev[s] for s in streams]
    m_row = [jnp.maximum(inter[s], jnp.max(dm[s], axis=-1, keepdims=True)) for s in streams]
    w_intra = [jnp.exp(dm[s] - m_row[s]) for s in streams]
    s_inter = [jnp.exp(inter[s] - m_row[s]) for s in streams]

    b_last = [b_row[s][:, L - 1:L] for s in streams]
    a_row = [b_last[s] - b_row[s] + li[s] for s in streams]
    m_new = [jnp.maximum(b_last[s] + m_prev[s], jnp.max(a_row[s], axis=-1, keepdims=True))
             for s in streams]
    s_state = [jnp.exp(b_last[s] + m_prev[s] - m_new[s]) for s in streams]
    e_col = [jnp.broadcast_to(jnp.exp(a_row[s] - m_new[s]), (L, L)).T[:, 0:1]
             for s in streams]

    nt = (((1,), (1,)), ((), ()))
    c_old = [c_ref[s] for s in streams]
    n_old = [n_ref[s] for s in streams]
    qk = [lax.dot_general(q_ref[qs(s)], k_ref[qs(s)], nt, preferred_element_type=F32)
          * scale * w_intra[s] for s in streams]
    q_c = [jnp.dot(q_ref[qs(s)], c_old[s].astype(BF16), preferred_element_type=F32) * scale
           for s in streams]
    num = [s_inter[s] * q_c[s] + jnp.dot(qk[s].astype(BF16), v_ref[vs(s)],
                                         preferred_element_type=F32) for s in streams]
    q_n = [jnp.sum(q_ref[qs(s)].astype(F32) * n_old[s], axis=-1, keepdims=True) * scale
           for s in streams]
    den = [s_inter[s] * q_n[s] + jnp.sum(qk[s], axis=-1, keepdims=True) for s in streams]
    for s in streams:
        hh = num[s] / jnp.maximum(jnp.abs(den[s]), jnp.exp(-m_row[s]))
        out_ref[vs(s)] = (jax.nn.sigmoid(o_ref[vs(s)].astype(F32))
                          * _rms(hh, hn_ref[hv(s)])).astype(out_ref.dtype)

    for s in streams:
        kw = k_ref[qs(s)].astype(F32) * e_col[s]
        c_ref[s] = s_state[s] * c_old[s] + lax.dot_general(
            kw.astype(BF16), v_ref[vs(s)], (((0,), (0,)), ((), ())),
            preferred_element_type=F32)
        n_ref[s] = s_state[s] * n_old[s] + jnp.sum(kw, axis=0, keepdims=True)
        m_ref[s] = jnp.broadcast_to(m_new[s], m_ref.shape[1:])


def mlstm_mixer(proj, gates, b_if, head_norm, batch, seq, q_col0, k_col0, v_col0, o_col0,
                dqk, dv, sides):
    L = MLSTM_CHUNK
    T = seq // L
    H = MLSTM_HEADS
    nb = 1
    qkw, vw = H * dqk, H * dv
    assert q_col0 % qkw == 0 and k_col0 % qkw == 0 and v_col0 % vw == 0 and o_col0 % vw == 0
    qb, kb, vb, ob = q_col0 // qkw, k_col0 // qkw, v_col0 // vw, o_col0 // vw
    proj3 = proj.reshape(batch, seq, proj.shape[1])
    gates_t = jnp.swapaxes(gates.reshape(batch, seq, 2 * H), 1, 2)
    rows = lambda cb: (lambda g, t: (g, t, cb))
    n_g = batch // nb
    side_in, side_out, side_shapes, side_srcs = _side_casts(
        sides, n_g * T, lambda g, t: g * T + t)
    out, *rounded = pl.pallas_call(
        _mlstm_kernel,
        grid=(n_g, T),
        in_specs=[
            pl.BlockSpec((2 * H, 1), lambda g, t: (0, 0)),
            pl.BlockSpec((nb, L, qkw), rows(qb)),
            pl.BlockSpec((nb, L, qkw), rows(kb)),
            pl.BlockSpec((nb, L, vw), rows(vb)),
            pl.BlockSpec((nb, L, vw), rows(ob)),
            pl.BlockSpec((nb, 2 * H, L), lambda g, t: (g, 0, t)),
            pl.BlockSpec((1, vw), lambda g, t: (0, 0)),
        ] + side_in,
        out_specs=[pl.BlockSpec((nb, L, vw), rows(0))] + side_out,
        out_shape=[jax.ShapeDtypeStruct((batch, seq, vw), BF16)] + side_shapes,
        scratch_shapes=[pltpu.VMEM((nb * H, dqk, dv), F32),
                        pltpu.VMEM((nb * H, 1, dqk), F32),
                        pltpu.VMEM((nb * H, 1, LANES), F32)],
        compiler_params=_params("arbitrary", "arbitrary"),
        name="mlstm",
    )(b_if.reshape(2 * H, 1), proj3, proj3, proj3, proj3, gates_t, head_norm.reshape(1, vw),
      *side_srcs)
    return (out.reshape(batch * seq, vw), *rounded)


def _conv_kernel(cb_ref, cc_ref, cx_ref, w_ref, o_ref, carry_ref):
    s = pl.program_id(2)
    ts = cb_ref.shape[0]

    @pl.when(s == 0)
    def _():
        carry_ref[...] = jnp.zeros_like(carry_ref)

    w = w_ref[...]
    w0, w1, w2 = w[0:1, :], w[1:2, :], w[2:3, :]
    u = cc_ref[...].astype(F32) * cx_ref[...].astype(F32)
    y = w2 * u + w1 * pltpu.roll(u, 1, axis=0) + w0 * pltpu.roll(u, 2, axis=0)
    o_ref[...] = (cb_ref[...].astype(F32) * y).astype(o_ref.dtype)

    prev = carry_ref[...]
    u_h = u[0:8, :]
    r = lax.broadcasted_iota(jnp.int32, u_h.shape, 0)
    u1 = jnp.where(r < 1, pltpu.roll(prev, 1, axis=0), pltpu.roll(u_h, 1, axis=0))
    u2 = jnp.where(r < 2, pltpu.roll(prev, 2, axis=0), pltpu.roll(u_h, 2, axis=0))
    y_h = w2 * u_h + w1 * u1 + w0 * u2
    o_ref[0:8, :] = (cb_ref[0:8, :].astype(F32) * y_h).astype(o_ref.dtype)
    carry_ref[...] = u[ts - 8:ts, :]


def gated_conv(proj, conv_w, batch, seq, cb_col0, cc_col0, cx_col0, ts=512, tc=512):
    C = conv_w.shape[1]
    nS = seq // ts
    b0, c0, x0 = cb_col0 // tc, cc_col0 // tc, cx_col0 // tc
    return pl.pallas_call(
        _conv_kernel,
        grid=(batch, C // tc, nS),
        in_specs=[
            pl.BlockSpec((ts, tc), lambda b, c, s: (b * nS + s, b0 + c)),
            pl.BlockSpec((ts, tc), lambda b, c, s: (b * nS + s, c0 + c)),
            pl.BlockSpec((ts, tc), lambda b, c, s: (b * nS + s, x0 + c)),
            pl.BlockSpec((CONV_K, tc), lambda b, c, s: (0, c)),
        ],
        out_specs=pl.BlockSpec((ts, tc), lambda b, c, s: (b * nS + s, c)),
        out_shape=jax.ShapeDtypeStruct((batch * seq, C), BF16),
        scratch_shapes=[pltpu.VMEM((8, tc), F32)],
        compiler_params=_params("parallel", "parallel", "arbitrary"),
        name="gated_conv",
    )(proj, proj, proj, conv_w)


def _xq_attn_kernel(a_ref, wq_ref, k_ref, v_ref, src_ref, o_ref, dst_ref):
    dst_ref[...] = src_ref[...].astype(dst_ref.dtype)
    dh = v_ref.shape[1]
    q = jnp.dot(a_ref[...], wq_ref[...], preferred_element_type=F32).astype(BF16)
    s = lax.dot_general(q, k_ref[...], (((1,), (1,)), ((), ())),
                        preferred_element_type=F32) * (dh ** -0.5)
    e = jnp.exp(s - jnp.max(s, axis=-1, keepdims=True))
    p = e / jnp.sum(e, axis=-1, keepdims=True)
    o_ref[...] = jnp.dot(p.astype(BF16), v_ref[...],
                         preferred_element_type=F32).astype(o_ref.dtype)


def xq_cross_attention(a, w_xq, k, v, seq, mem_len, dh, tm, side_src, side_lead):
    M, K = a.shape
    H = XATTN_HEADS
    assert M % tm == 0 and seq % tm == 0 and w_xq.shape == (K, H * dh)
    tiles_per_batch = seq // tm
    n_m = M // tm
    side_in, side_out, side_shape = _side_cast_specs(
        side_src, side_lead, n_m * H, lambda i, j: i * H + j)
    kv_spec = pl.BlockSpec((mem_len, dh), lambda i, j: (i // tiles_per_batch, j))
    return pl.pallas_call(
        _xq_attn_kernel,
        grid=(n_m, H),
        in_specs=[pl.BlockSpec((tm, K), lambda i, j: (i, 0)),
                  pl.BlockSpec((K, dh), lambda i, j: (0, j)),
                  kv_spec, kv_spec, side_in],
        out_specs=[pl.BlockSpec((tm, dh), lambda i, j: (i, j)), side_out],
        out_shape=[jax.ShapeDtypeStruct((M, H * dh), BF16), side_shape],
        compiler_params=_params("arbitrary", "arbitrary"),
        name="xq_cross_attention",
    )(a, w_xq, k, v, side_src)


def kernel(x, mem, norm_pre_mix, norm_post_mix, w_in, b_if, mlstm_head_norm, conv_w, w_branch, w_mix_out, norm_pre_xattn, norm_post_xattn, norm_mem, w_xq, w_xk, w_xv, w_xo, norm_pre_ffn, norm_post_ffn, w_ffn_gate, w_ffn_up, w_ffn_down):
    B, S, D = x.shape
    mem_len = mem.shape[1]
    depth = w_in.shape[0]
    H = MLSTM_HEADS
    v_width = mlstm_head_norm.shape[1]
    conv_width = conv_w.shape[2]
    dv = v_width // H
    dqk = dv // 2
    qk_width = H * dqk
    dh = w_xq.shape[2] // XATTN_HEADS

    in_width = w_in.shape[2]
    w_in_t = jnp.swapaxes(w_in, 1, 2).reshape(depth * in_width, D)
    q0 = 0
    k0 = q0 + qk_width
    v0 = k0 + qk_width
    o0 = v0 + v_width
    gate0 = o0 + v_width
    cb0 = gate0
    cc0 = cb0 + conv_width
    cx0 = cc0 + conv_width
    g0 = cx0 + conv_width

    xr = x.reshape(B * S, D)
    memr = mem.reshape(B * mem_len, D)
    for l in range(depth):
        h = prenorm(xr, norm_pre_mix[l])
        proj, gates, w_mix_bf, w_xq_bf, w_gate_bf = input_projection(
            h, w_in_t, l * in_width, gate0, 2 * H, in_width - 2 * H, 1024, 1024,
            [(w_mix_out, (l,)), (w_xq, (l,)), (w_ffn_gate, (l,))])
        h_a, w_up_bf = mlstm_mixer(proj, gates, b_if[l], mlstm_head_norm[l], B, S,
                                   q0, k0, v0, o0, dqk, dv, [(w_ffn_up, (l,))])
        h_b = gated_conv(proj, conv_w[l], B, S, cb0, cc0, cx0, ts=2048, tc=1024)
        merged = branch_merge(h_a, h_b, w_branch, (l,), proj, g0, 2048, 512)
        xr, h = matmul_post_pre_norm(merged, w_mix_bf, xr,
                                     norm_post_mix[l], norm_pre_xattn[l], 1024, 512)

        m = prenorm(memr, norm_mem[l])
        k = matmul(m, w_xk, BF16, B * mem_len, 512, lead=(l,))
        v = matmul(m, w_xv, BF16, B * mem_len, 512, lead=(l,))
        o, w_xo_bf = xq_cross_attention(h, w_xq_bf, k, v, S, mem_len, dh, 1024, w_xo, (l,))
        xr, h = matmul_post_pre_norm(o, w_xo_bf, xr,
                                     norm_post_xattn[l], norm_pre_ffn[l], 1024, 512)

        hidden, w_down_bf = swiglu_up(h, w_gate_bf, w_up_bf, 4096, 256, w_ffn_down, (l,))
        y = matmul_ksplit(hidden, w_down_bf, BF16, 1024, 1024, w_down_bf.shape[0] // 2)
        xr = post_norm_residual(y, xr, norm_post_ffn[l])
    return xr.reshape(B, S, D)
```
